```python
import jax, jax.numpy as jnp
from jax import lax
import numpy as np

D_MODEL = 1024
BATCH = 32
SEQ = 256
DEPTH = 2
DEC_BATCH = 4
DEC_SEQ = 2048
PAST_LEN = 512

GRID_W = 64
N_MIXERS = 2
N_MLSTM = (DEPTH + 1) // 2
N_DELTA = DEPTH // 2
N_HEADS = 8
HEAD_DIM = D_MODEL // N_HEADS
D_INNER = N_HEADS * HEAD_DIM
N_GATE = 4 * N_HEADS
IN_DIM = 4 * D_INNER + N_GATE
CHUNK = 64
CONV_W = 5
N_EXPERTS = 16
CAP_FACTOR = 2
D_EXPERT = 1024
ADA_DIM = 6 * D_MODEL
EPS = 1e-6

kernel_name = 'hybrid_mlstm_gdn_ec_moe_diffusion_step'


def rmsnorm(x, g):
    xf = x.astype(jnp.float32)
    y = xf * lax.rsqrt(jnp.mean(xf * xf, -1, keepdims=True) + EPS)
    return (y * g.astype(jnp.float32)).astype(x.dtype)


def head_rmsnorm(x, g):
    return x * lax.rsqrt(jnp.mean(x * x, -1, keepdims=True) + EPS) * g.astype(jnp.float32)


def l2norm(x):
    return x * lax.rsqrt(jnp.sum(x * x, -1, keepdims=True) + EPS)


def to_heads(t):
    B, T, _ = t.shape
    return t.reshape(B, T, N_HEADS, HEAD_DIM).transpose(0, 2, 1, 3)


def from_heads(t):
    B, H, T, Dh = t.shape
    return t.transpose(0, 2, 1, 3).reshape(B, T, H * Dh)


def split_gates(p):
    B, T, _ = p.shape
    return jnp.moveaxis(p.reshape(B, T, 2, 2, N_HEADS), 1, -1)


def rev(t):
    return jnp.flip(t, 2)


def short_conv(x, w):
    pad = CONV_W // 2
    return lax.conv_general_dilated(x, w[:, None, :].astype(x.dtype), window_strides=(1,),
                                    padding=[(pad, pad)], dimension_numbers=('NWC', 'WIO', 'NWC'),
                                    feature_group_count=x.shape[-1])


def mlstm_dir(q, k, v, ig, lf, C0, n0, m0):
    B, H, T, _ = q.shape
    nc = T // CHUNK
    causal = jnp.tril(jnp.ones((CHUNK, CHUNK), bool))

    def split(t):
        return jnp.moveaxis(t.reshape(B, H, nc, CHUNK, *t.shape[3:]), 2, 0)

    def step(carry, xs):
        C, n, m = carry
        qc, kc, vc, ic, fc = xs
        b = jnp.cumsum(fc, -1)
        logd = jnp.where(causal, b[..., :, None] - b[..., None, :] + ic[..., None, :], -jnp.inf)
        inter = b + m[..., None]
        mt = jnp.maximum(inter, jnp.max(logd, -1))
        s = jnp.einsum('bhtd,bhsd->bhts', qc, kc) * jnp.exp(logd - mt[..., None])
        sc = jnp.exp(inter - mt)
        num = jnp.einsum('bhts,bhsv->bhtv', s, vc) + sc[..., None] * jnp.einsum('bhtd,bhdv->bhtv', qc, C)
        den = jnp.sum(s, -1) + sc * jnp.einsum('bhtd,bhd->bht', qc, n)
        h = num / jnp.maximum(jnp.abs(den), jnp.exp(-mt))[..., None]
        bl = b[..., -1]
        logw = bl[..., None] - b + ic
        m_new = jnp.maximum(bl + m, jnp.max(logw, -1))
        w = jnp.exp(logw - m_new[..., None])
        dec = jnp.exp(bl + m - m_new)
        C_new = dec[..., None, None] * C + jnp.einsum('bhs,bhsd,bhsv->bhdv', w, kc, vc)
        n_new = dec[..., None] * n + jnp.einsum('bhs,bhsd->bhd', w, kc)
        return (C_new, n_new, m_new), h

    carry0 = (C0.astype(jnp.float32), n0.astype(jnp.float32), m0.astype(jnp.float32))
    (C, n, m), h = lax.scan(step, carry0, tuple(split(t) for t in (q, k, v, ig, lf)))
    return jnp.moveaxis(h, 0, 2).reshape(B, H, T, -1), C, n, m


def delta_dir(q, k, v, g, beta, S0):
    B, H, T, _ = q.shape
    nc = T // CHUNK

    def split(t):
        return t.reshape(B, H, nc, CHUNK, *t.shape[3:])

    q, k, v, g, beta = split(q), split(k), split(v), split(g), split(beta)
    lower = jnp.tril(jnp.ones((CHUNK, CHUNK), bool))
    strict = jnp.tril(jnp.ones((CHUNK, CHUNK), bool), -1)
    G = jnp.cumsum(g, -1)
    decay = jnp.exp(jnp.where(lower, G[..., :, None] - G[..., None, :], -jnp.inf))
    kk = jnp.einsum('bhntd,bhnsd->bhnts', k, k)
    M = jnp.eye(CHUNK, dtype=jnp.float32) + jnp.where(strict, kk * decay * beta[..., :, None], 0.0)
    eG = jnp.exp(G)
    U = lax.linalg.triangular_solve(M, v * beta[..., None], left_side=True, lower=True, unit_diagonal=True)
    Wk = lax.linalg.triangular_solve(M, k * (beta * eG)[..., None], left_side=True, lower=True, unit_diagonal=True)
    qk = jnp.einsum('bhntd,bhnsd->bhnts', q, k) * decay
    kd = k * jnp.exp(G[..., -1:] - G)[..., None]
    eGl = eG[..., -1]

    def step(S, xs):
        qc, kdc, Uc, Wkc, qkc, eGc, eGlc = xs
        W = Uc - jnp.einsum('bhld,bhdv->bhlv', Wkc, S)
        o = eGc[..., None] * jnp.einsum('bhld,bhdv->bhlv', qc, S) + jnp.einsum('bhts,bhsv->bhtv', qkc, W)
        S_new = eGlc[..., None, None] * S + jnp.einsum('bhld,bhlv->bhdv', kdc, W)
        return S_new, o

    xs = tuple(jnp.moveaxis(t, 2, 0) for t in (q, kd, U, Wk, qk, eG, eGl))
    S, o = lax.scan(step, S0.astype(jnp.float32), xs)
    return jnp.moveaxis(o, 0, 2).reshape(B, H, T, -1), S


def mlstm_mixer(h, w_in, w_out, ig_b, fg_b, norm_g, C0, n0, m0):
    p = (h @ w_in).astype(jnp.float32)
    q = to_heads(p[..., :D_INNER])
    k = to_heads(p[..., D_INNER:2 * D_INNER]) * HEAD_DIM ** -0.5
    v = to_heads(p[..., 2 * D_INNER:3 * D_INNER])
    o = to_heads(p[..., 3 * D_INNER:4 * D_INNER])
    gates = split_gates(p[..., 4 * D_INNER:])
    ig = gates[:, :, 0] + ig_b.astype(jnp.float32)[None, :, :, None]
    lf = jax.nn.log_sigmoid(gates[:, :, 1] + fg_b.astype(jnp.float32)[None, :, :, None])
    hf, Cf, nf, mf = mlstm_dir(q, k, v, ig[:, 0], lf[:, 0], C0[:, 0], n0[:, 0], m0[:, 0])
    hb, Cb, nb, mb = mlstm_dir(rev(q), rev(k), rev(v), rev(ig[:, 1]), rev(lf[:, 1]), C0[:, 1], n0[:, 1], m0[:, 1])
    hn = head_rmsnorm(hf + rev(hb), norm_g.reshape(N_HEADS, 1, HEAD_DIM))
    y = from_heads(jax.nn.sigmoid(o) * hn).astype(h.dtype) @ w_out
    return y, (jnp.stack([Cf, Cb], 1), jnp.stack([nf, nb], 1), jnp.stack([mf, mb], 1))


def delta_mixer(h, w_in, w_out, conv_w, A_log, dt_bias, norm_g, S0, rows):
    B, T, _ = h.shape
    p = h @ w_in
    qkv = short_conv(p[..., :3 * D_INNER].reshape(B * rows, T // rows, 3 * D_INNER), conv_w)
    qkv = jax.nn.silu(qkv).reshape(B, T, 3 * D_INNER).astype(jnp.float32)
    q = l2norm(to_heads(qkv[..., :D_INNER])) * HEAD_DIM ** -0.5
    k = l2norm(to_heads(qkv[..., D_INNER:2 * D_INNER]))
    v = to_heads(qkv[..., 2 * D_INNER:])
    z = to_heads(p[..., 3 * D_INNER:4 * D_INNER].astype(jnp.float32))
    gates = split_gates(p[..., 4 * D_INNER:].astype(jnp.float32))
    g = -jnp.exp(A_log.astype(jnp.float32))[None, :, :, None] * jax.nn.softplus(
        gates[:, :, 0] + dt_bias.astype(jnp.float32)[None, :, :, None])
    beta = jax.nn.sigmoid(gates[:, :, 1])
    of, Sf = delta_dir(q, k, v, g[:, 0], beta[:, 0], S0[:, 0])
    ob, Sb = delta_dir(rev(q), rev(k), rev(v), rev(g[:, 1]), rev(beta[:, 1]), S0[:, 1])
    o = head_rmsnorm(of + rev(ob), norm_g) * jax.nn.silu(z)
    y = from_heads(o).astype(h.dtype) @ w_out
    return y, jnp.stack([Sf, Sb], 1)


def expert_choice_ffn(x, w_router, w_gate, w_up, w_down):
    B, T, _ = x.shape
    cap = CAP_FACTOR * T // N_EXPERTS
    aff = jax.nn.softmax((x @ w_router).astype(jnp.float32), -1)
    gsel, idx = lax.top_k(jnp.swapaxes(aff, 1, 2), cap)
    bidx = jnp.arange(B)[:, None, None]
    xs = x[bidx, idx]
    hdn = jax.nn.silu(jnp.einsum('becd,edf->becf', xs, w_gate)) * jnp.einsum('becd,edf->becf', xs, w_up)
    out = jnp.einsum('becf,efd->becd', hdn, w_down) * gsel[..., None].astype(x.dtype)
    return jnp.zeros_like(x).at[bidx, idx].add(out)


def modulation(cond, w_ada, b_ada):
    m = jax.nn.silu(cond) @ w_ada + b_ada
    return jnp.split(m[:, None, :], 6, axis=-1)


def trunk(x, cond, rows, mC, mn, mm, dS, params):
    (norm1_g, norm2_g, w_ada, b_ada, w_in, w_out, w_router, w_gate, w_up, w_down, final_g,
     mlstm_ig_b, mlstm_fg_b, mlstm_norm_g, delta_conv_w, delta_A_log, delta_dt_bias, delta_norm_g) = params
    m_states, d_states = [], []
    for i in range(DEPTH):
        sh1, sc1, g1, sh2, sc2, g2 = modulation(cond, w_ada[i], b_ada[i])
        h = rmsnorm(x, norm1_g[i]) * (1 + sc1) + sh1
        j = i // N_MIXERS
        if i % N_MIXERS == 0:
            y, st = mlstm_mixer(h, w_in[i], w_out[i], mlstm_ig_b[j], mlstm_fg_b[j], mlstm_norm_g[j],
                                mC[:, j], mn[:, j], mm[:, j])
            m_states.append(st)
        else:
            y, st = delta_mixer(h, w_in[i], w_out[i], delta_conv_w[j], delta_A_log[j], delta_dt_bias[j],
                                delta_norm_g[j], dS[:, j], rows)
            d_states.append(st)
        x = x + g1 * y
        h = rmsnorm(x, norm2_g[i]) * (1 + sc2) + sh2
        x = x + g2 * expert_choice_ffn(h, w_router[i], w_gate[i], w_up[i], w_down[i])
    new_C = jnp.stack([s[0] for s in m_states], 1)
    new_n = jnp.stack([s[1] for s in m_states], 1)
    new_m = jnp.stack([s[2] for s in m_states], 1)
    new_S = jnp.stack(d_states, 1)
    return rmsnorm(x, final_g), new_C, new_n, new_m, new_S


def setup_inputs(seed: int = 0) -> dict:
    key = jax.random.key(seed)
    ks = jax.random.split(key, 28)
    f32 = jnp.float32
    D = D_MODEL

    def nrm(k, shape, s):
        return jax.random.normal(k, shape, f32) * s

    dt = jnp.exp(jax.random.uniform(ks[25], (N_DELTA, 2, N_HEADS), f32, np.log(1e-3), np.log(1e-1)))
    return {
        'x_prompt': nrm(ks[0], (BATCH, SEQ, D), 1.0),
        'x_sample': nrm(ks[1], (DEC_BATCH, DEC_SEQ, D), 1.0),
        'state_mlstm_C': nrm(ks[2], (DEC_BATCH, N_MLSTM, 2, N_HEADS, HEAD_DIM, HEAD_DIM), 0.1),
        'state_mlstm_n': nrm(ks[3], (DEC_BATCH, N_MLSTM, 2, N_HEADS, HEAD_DIM), 0.1),
        'state_mlstm_m': nrm(ks[4], (DEC_BATCH, N_MLSTM, 2, N_HEADS), 1.0),
        'state_delta_S': nrm(ks[5], (DEC_BATCH, N_DELTA, 2, N_HEADS, HEAD_DIM, HEAD_DIM), 0.1),
        'c': nrm(ks[6], (DEC_BATCH, D), 1.0),
        'c_ctx': nrm(ks[7], (D,), 1.0),
        'norm1_g': 1.0 + nrm(ks[8], (DEPTH, D), 0.02),
        'norm2_g': 1.0 + nrm(ks[9], (DEPTH, D), 0.02),
        'w_ada': nrm(ks[10], (DEPTH, D, ADA_DIM), 0.5 * D ** -0.5),
        'b_ada': nrm(ks[11], (DEPTH, ADA_DIM), 0.02),
        'w_in': nrm(ks[12], (DEPTH, D, IN_DIM), D ** -0.5),
        'w_out': nrm(ks[13], (DEPTH, D_INNER, D), D_INNER ** -0.5),
        'w_router': nrm(ks[14], (DEPTH, D, N_EXPERTS), D ** -0.5),
        'w_gate': nrm(ks[15], (DEPTH, N_EXPERTS, D, D_EXPERT), D ** -0.5),
        'w_up': nrm(ks[16], (DEPTH, N_EXPERTS, D, D_EXPERT), D ** -0.5),
        'w_down': nrm(ks[17], (DEPTH, N_EXPERTS, D_EXPERT, D), D_EXPERT ** -0.5),
        'final_g': 1.0 + nrm(ks[18], (D,), 0.02),
        'mlstm_ig_b': nrm(ks[19], (N_MLSTM, 2, N_HEADS), 0.1),
        'mlstm_fg_b': jax.random.uniform(ks[20], (N_MLSTM, 2, N_HEADS), f32, 3.0, 6.0),
        'mlstm_norm_g': 1.0 + nrm(ks[21], (N_MLSTM, D_INNER), 0.02),
        'delta_conv_w': nrm(ks[22], (N_DELTA, CONV_W, 3 * D_INNER), CONV_W ** -0.5),
        'delta_A_log': jnp.log(jax.random.uniform(ks[23], (N_DELTA, 2, N_HEADS), f32, 1.0, 16.0)),
        'delta_dt_bias': dt + jnp.log(-jnp.expm1(-dt)),
        'delta_norm_g': 1.0 + nrm(ks[24], (N_DELTA, HEAD_DIM), 0.02),
    }


def reference(x_prompt, x_sample, state_mlstm_C, state_mlstm_n, state_mlstm_m, state_delta_S, c, c_ctx,
              norm1_g, norm2_g, w_ada, b_ada, w_in, w_out, w_router, w_gate, w_up, w_down, final_g,
              mlstm_ig_b, mlstm_fg_b, mlstm_norm_g, delta_conv_w, delta_A_log, delta_dt_bias, delta_norm_g):
    params = (norm1_g, norm2_g, w_ada, b_ada, w_in, w_out, w_router, w_gate, w_up, w_down, final_g,
              mlstm_ig_b, mlstm_fg_b, mlstm_norm_g, delta_conv_w, delta_A_log, delta_dt_bias, delta_norm_g)
    f32 = jnp.float32
    nb = x_prompt.shape[0]
    zC = jnp.zeros((nb, N_MLSTM, 2, N_HEADS, HEAD_DIM, HEAD_DIM), f32)
    zn = jnp.zeros((nb, N_MLSTM, 2, N_HEADS, HEAD_DIM), f32)
    zm = jnp.zeros((nb, N_MLSTM, 2, N_HEADS), f32)
    zS = jnp.zeros((nb, N_DELTA, 2, N_HEADS, HEAD_DIM, HEAD_DIM), f32)
    y_prompt, new_mlstm_C, new_mlstm_n, new_mlstm_m, new_delta_S = trunk(
        x_prompt, c_ctx[None, :], 1, zC, zn, zm, zS, params)
    rows = x_sample.shape[1] // GRID_W
    y_sample = trunk(x_sample, c, rows, state_mlstm_C, state_mlstm_n, state_mlstm_m, state_delta_S, params)[0]
    return (y_prompt, y_sample, new_mlstm_C, new_mlstm_n, new_mlstm_m, new_delta_S)
```

```python
import functools
import math

import jax
import jax.numpy as jnp
from jax import lax
from jax.experimental import pallas as pl
from jax.experimental.pallas import tpu as pltpu

F32 = jnp.float32
BF16 = jnp.bfloat16
I32 = jnp.int32

N_HEAD = 8
HEAD = 128
N_EXP = 16
CAP_FACTOR = 2
GRID_W = 64
CONV_W = 5
DELTA_CHUNK = 64
MLSTM_CHUNK = 256
EPS = 1e-6
ROW_TILE = 256
VMEM_LIMIT = 56 * 1024 * 1024


def _cparams(*sem):
    return pltpu.CompilerParams(dimension_semantics=sem, vmem_limit_bytes=VMEM_LIMIT)


def _dot(a, b):
    return jnp.dot(a, b, preferred_element_type=F32)


def _dot_nt(a, b):
    return lax.dot_general(a, b, (((1,), (1,)), ((), ())), preferred_element_type=F32)


def _dot_tn(a, b):
    return lax.dot_general(a, b, (((0,), (0,)), ((), ())), preferred_element_type=F32)


def _split(x):
    hi = x.astype(BF16)
    lo = (x - hi.astype(F32)).astype(BF16)
    return hi, lo


def _dot3(a, b, dot=_dot):
    ah, al = _split(a)
    bh, bl = _split(b)
    return dot(ah, bh) + (dot(ah, bl) + dot(al, bh))


def _sigmoid(x):
    return 1.0 / (1.0 + jnp.exp(-x))


def _silu(x):
    return x * _sigmoid(x)


def _softplus(x):
    return jnp.maximum(x, 0.0) + jnp.log1p(jnp.exp(-jnp.abs(x)))


def _log_sigmoid(x):
    return -_softplus(-x)


def _rms_mod(x, g, scale, shift):
    y = x * lax.rsqrt(jnp.mean(x * x, -1, keepdims=True) + EPS)
    return (y * g) * (1.0 + scale) + shift


def _mod_kernel(cond_ref, w_ref, b_ref, o_ref):
    c = cond_ref[...]
    o_ref[0] = _dot3(_silu(c), w_ref[0]) + b_ref[0]


def _modulation(cond8, w_ada, b_ada):
    depth, d, n6 = w_ada.shape
    tn = 1536
    return pl.pallas_call(
        _mod_kernel,
        grid=(depth, n6 // tn),
        in_specs=[pl.BlockSpec((8, d), lambda l, j: (0, 0)),
                  pl.BlockSpec((1, d, tn), lambda l, j: (l, 0, j)),
                  pl.BlockSpec((1, 1, tn), lambda l, j: (l, 0, j))],
        out_specs=pl.BlockSpec((1, 8, tn), lambda l, j: (l, 0, j)),
        out_shape=jax.ShapeDtypeStruct((depth, 8, n6), F32),
        compiler_params=_cparams("arbitrary", "arbitrary"),
        name="modulation",
    )(cond8, w_ada, b_ada.reshape(depth, 1, n6))


def _in_kernel(x_ref, mod_ref, ng_ref, w_ref, wg_ref, *rest, kind, conv_row):
    if kind == "delta":
        cw_ref, q_ref, k_ref, v_ref, og_ref, gates_ref = rest
    else:
        q_ref, k_ref, v_ref, og_ref, gates_ref = rest
    x = x_ref[0]
    tm, d = x.shape
    h = _rms_mod(x, ng_ref[...], mod_ref[0, 1:2, :], mod_ref[0, 0:1, :])
    hb = h.astype(BF16)
    gates_ref[0] = _dot3(h, wg_ref[...])
    if kind == "delta":
        pos = lax.broadcasted_iota(I32, (tm, 1), 0) % conv_row
    for j in range(4):
        acc = _dot(hb, w_ref[:, j * d:(j + 1) * d])
        if kind == "mlstm":
            if j == 1:
                acc = acc * (HEAD ** -0.5)
            if j == 3:
                og_ref[0] = _sigmoid(acc).astype(BF16)
                continue
        else:
            if j == 3:
                og_ref[0] = _silu(acc).astype(BF16)
                continue
            cw = cw_ref[:, j * d:(j + 1) * d]
            out = acc * cw[2:3, :]
            for tap, sft in ((0, -2), (1, -1), (3, 1), (4, 2)):
                shifted = pltpu.roll(acc, (-sft) % tm, 0)
                valid = jnp.logical_and(pos + sft >= 0, pos + sft < conv_row)
                out = out + jnp.where(valid, shifted, 0.0) * cw[tap:tap + 1, :]
            acc = _silu(out)
        dst = (q_ref, k_ref, v_ref)[j]
        for hh in range(N_HEAD):
            blk = acc[:, hh * HEAD:(hh + 1) * HEAD]
            if kind == "delta" and j < 2:
                blk = blk * lax.rsqrt(jnp.sum(blk * blk, -1, keepdims=True) + EPS)
                if j == 0:
                    blk = blk * (HEAD ** -0.5)
            dst[0, hh] = blk.astype(BF16)


def _in_proj(x, mod, norm_g, w_bf, w_gate, conv_w, kind, conv_row):
    b, t, d = x.shape
    tm = ROW_TILE
    nt = t // tm
    bc = mod.shape[0]
    mod_map = (lambda i, j: (i, 0, 0)) if bc > 1 else (lambda i, j: (0, 0, 0))
    in_specs = [pl.BlockSpec((1, tm, d), lambda i, j: (i, j, 0)),
                pl.BlockSpec((1, 8, d), mod_map),
                pl.BlockSpec((1, d), lambda i, j: (0, 0)),
                pl.BlockSpec((d, 4 * d), lambda i, j: (0, 0)),
                pl.BlockSpec((d, 32), lambda i, j: (0, 0))]
    args = [x, mod, norm_g.reshape(1, d), w_bf, w_gate]
    if kind == "delta":
        in_specs.append(pl.BlockSpec((CONV_W, 3 * d), lambda i, j: (0, 0)))
        args.append(conv_w)
    hd_spec = pl.BlockSpec((1, N_HEAD, tm, HEAD), lambda i, j: (i, 0, j, 0))
    hd_shape = jax.ShapeDtypeStruct((b, N_HEAD, t, HEAD), BF16)
    return pl.pallas_call(
        functools.partial(_in_kernel, kind=kind, conv_row=conv_row),
        grid=(b, nt),
        in_specs=in_specs,
        out_specs=[hd_spec, hd_spec, hd_spec,
                   pl.BlockSpec((1, tm, d), lambda i, j: (i, j, 0)),
                   pl.BlockSpec((1, tm, 32), lambda i, j: (i, j, 0))],
        out_shape=[hd_shape, hd_shape, hd_shape,
                   jax.ShapeDtypeStruct((b, t, d), BF16),
                   jax.ShapeDtypeStruct((b, t, 32), F32)],
        compiler_params=_cparams("arbitrary", "arbitrary"),
        name="in_proj_" + kind,
    )(*args)


def _gate_layouts(gates, nc, chunk):
    b, t, _ = gates.shape
    g = gates.reshape(b, nc, chunk, 4, N_HEAD)
    return g.transpose(0, 4, 1, 2, 3), g.transpose(0, 4, 1, 3, 2)


def _mlstm_kernel(*refs, hb, nc, chunk, zero_init, emit_state):
    it = iter(refs)
    q_ref, k_ref, v_ref, og_ref, gcol_ref, grow_ref, igb_ref, fgb_ref, ng_ref = (next(it) for _ in range(9))
    if not zero_init:
        c0_ref, n0_ref, m0_ref = (next(it) for _ in range(3))
    a_ref = next(it)
    if emit_state:
        cn_ref, nn_ref, mn_ref = (next(it) for _ in range(3))
    c_s, n_s, m_s, hf_s, hr_s = (next(it) for _ in range(5))
    L = chunk
    hg = pl.program_id(1)
    if zero_init:
        c_s[...] = jnp.zeros_like(c_s)
        n_s[...] = jnp.zeros_like(n_s)
        m_s[...] = jnp.zeros_like(m_s)
    else:
        c_s[...] = c0_ref[0]
        n_s[...] = n0_ref[0]
        m_s[...] = m0_ref[0]
    row = lax.broadcasted_iota(I32, (L, L), 0)
    col = lax.broadcasted_iota(I32, (L, L), 1)
    tris = (row >= col, col >= row)

    def body(c, carry):
        for d in range(2):
            cc = c if d == 0 else nc - 1 - c
            tri = tris[d]
            for hh in range(hb):
                head = hg * hb + hh
                qc = q_ref[0, hh, cc]
                kc = k_ref[0, hh, cc]
                vc = v_ref[0, hh, cc]
                gcol = gcol_ref[0, hh, cc]
                grw = grow_ref[0, hh, cc]
                igb = igb_ref[d, head]
                fgb = fgb_ref[d, head]
                i_col = gcol[:, 2 * d:2 * d + 1] + igb
                f_col = _log_sigmoid(gcol[:, 2 * d + 1:2 * d + 2] + fgb)
                i_row = grw[2 * d:2 * d + 1, :] + igb
                f_row = _log_sigmoid(grw[2 * d + 1:2 * d + 2, :] + fgb)
                b_col = jnp.sum(jnp.where(tri, f_row, 0.0), axis=1, keepdims=True)
                b_row = jnp.sum(jnp.where(tris[1 - d], f_col, 0.0), axis=0, keepdims=True)
                bl = jnp.sum(f_row, axis=1, keepdims=True)
                m_prev = m_s[d, hh]
                c_prev = c_s[d, hh]
                n_prev = n_s[d, hh]
                logd = jnp.where(tri, b_col - b_row + i_row, -jnp.inf)
                inter = b_col + m_prev
                mt = jnp.maximum(inter, jnp.max(logd, axis=1, keepdims=True))
                s = _dot_nt(qc, kc) * jnp.exp(logd - mt)
                sc = jnp.exp(inter - mt)
                num = _dot(s.astype(BF16), vc) + sc * _dot(qc, c_prev.astype(BF16))
                den = jnp.sum(s, axis=1, keepdims=True) + sc * jnp.sum(qc.astype(F32) * n_prev, axis=1, keepdims=True)
                hout = num / jnp.maximum(jnp.abs(den), jnp.exp(-mt))
                if d == 0:
                    hf_s[hh, cc] = hout
                else:
                    hr_s[hh, cc] = hout
                logw = bl - b_col + i_col
                m_new = jnp.maximum(bl + m_prev, jnp.max(logw, axis=0, keepdims=True))
                w = jnp.exp(logw - m_new)
                dec = jnp.exp(bl + m_prev - m_new)
                kw = kc.astype(F32) * w
                c_s[d, hh] = dec * c_prev + _dot_tn(kw.astype(BF16), vc)
                n_s[d, hh] = dec * n_prev + jnp.sum(kw, axis=0, keepdims=True)
                m_s[d, hh] = m_new
        return carry

    lax.fori_loop(0, nc, body, 0)

    def emit(c, carry):
        t0 = pl.multiple_of(c * L, L)
        for hh in range(hb):
            tot = hf_s[hh, c] + hr_s[hh, c]
            hn = tot * lax.rsqrt(jnp.mean(tot * tot, -1, keepdims=True) + EPS) * ng_ref[:, hh * HEAD:(hh + 1) * HEAD]
            gate = og_ref[0, pl.ds(t0, L), hh * HEAD:(hh + 1) * HEAD].astype(F32)
            a_ref[0, pl.ds(t0, L), hh * HEAD:(hh + 1) * HEAD] = (gate * hn).astype(BF16)
        return carry

    lax.fori_loop(0, nc, emit, 0)
    if emit_state:
        cn_ref[0] = c_s[...]
        nn_ref[0] = n_s[...]
        mn_ref[0] = m_s[...]


def _mlstm(q, k, v, og, gates, ig_b, fg_b, norm_g, state, emit_state):
    b, _, t, _ = q.shape
    d = N_HEAD * HEAD
    chunk = min(t, MLSTM_CHUNK)
    nc = t // chunk
    hb = 4
    gcol, grow = _gate_layouts(gates, nc, chunk)
    qkv_spec = pl.BlockSpec((1, hb, nc, chunk, HEAD), lambda i, j: (i, j, 0, 0, 0))
    r5 = lambda a: a.reshape(b, N_HEAD, nc, chunk, HEAD)
    smem = pl.BlockSpec(memory_space=pltpu.SMEM)
    in_specs = [qkv_spec, qkv_spec, qkv_spec,
                pl.BlockSpec((1, t, hb * HEAD), lambda i, j: (i, 0, j)),
                pl.BlockSpec((1, hb, nc, chunk, 4), lambda i, j: (i, j, 0, 0, 0)),
                pl.BlockSpec((1, hb, nc, 4, chunk), lambda i, j: (i, j, 0, 0, 0)),
                smem, smem,
                pl.BlockSpec((1, hb * HEAD), lambda i, j: (0, j))]
    args = [r5(q), r5(k), r5(v), og, gcol, grow, ig_b, fg_b, norm_g.reshape(1, d)]
    zero_init = state is None
    c_spec = pl.BlockSpec((1, 2, hb, HEAD, HEAD), lambda i, j: (i, 0, j, 0, 0))
    n_spec = pl.BlockSpec((1, 2, hb, 1, HEAD), lambda i, j: (i, 0, j, 0, 0))
    m_spec = pl.BlockSpec((1, 2, hb, 1, 1), lambda i, j: (i, 0, j, 0, 0))
    if not zero_init:
        c0, n0, m0 = state
        in_specs += [c_spec, n_spec, m_spec]
        args += [c0, n0.reshape(b, 2, N_HEAD, 1, HEAD), m0.reshape(b, 2, N_HEAD, 1, 1)]
    out_specs = [pl.BlockSpec((1, t, hb * HEAD), lambda i, j: (i, 0, j))]
    out_shape = [jax.ShapeDtypeStruct((b, t, d), BF16)]
    if emit_state:
        out_specs += [c_spec, n_spec, m_spec]
        out_shape += [jax.ShapeDtypeStruct((b, 2, N_HEAD, HEAD, HEAD), F32),
                      jax.ShapeDtypeStruct((b, 2, N_HEAD, 1, HEAD), F32),
                      jax.ShapeDtypeStruct((b, 2, N_HEAD, 1, 1), F32)]
    res = pl.pallas_call(
        functools.partial(_mlstm_kernel, hb=hb, nc=nc, chunk=chunk, zero_init=zero_init, emit_state=emit_state),
        grid=(b, N_HEAD // hb),
        in_specs=in_specs,
        out_specs=out_specs,
        out_shape=out_shape,
        scratch_shapes=[pltpu.VMEM((2, hb, HEAD, HEAD), F32),
                        pltpu.VMEM((2, hb, 1, HEAD), F32),
                        pltpu.VMEM((2, hb, 1, 1), F32),
                        pltpu.VMEM((hb, nc, chunk, HEAD), F32),
                        pltpu.VMEM((hb, nc, chunk, HEAD), F32)],
        compiler_params=_cparams("arbitrary", "arbitrary"),
        name="mlstm",
    )(*args)
    if emit_state:
        a, cn, nn, mn = res
        return a, (cn, nn.reshape(b, 2, N_HEAD, HEAD), mn.reshape(b, 2, N_HEAD))
    return res[0], None


def _delta_kernel(*refs, hb, nc, chunk, zero_init, emit_state):
    it = iter(refs)
    q_ref, k_ref, v_ref, zs_ref, gcol_ref, grow_ref, alog_ref, dtb_ref, ng_ref = (next(it) for _ in range(9))
    if not zero_init:
        s0_ref = next(it)
    a_ref = next(it)
    if emit_state:
        sn_ref = next(it)
    s_s, of_s, or_s = (next(it) for _ in range(3))
    L = chunk
    hg = pl.program_id(1)
    if zero_init:
        s_s[...] = jnp.zeros_like(s_s)
    else:
        s_s[...] = s0_ref[0]
    row = lax.broadcasted_iota(I32, (L, L), 0)
    col = lax.broadcasted_iota(I32, (L, L), 1)
    tris = (row >= col, col >= row)
    stricts = (row > col, col > row)
    eye = jnp.where(row == col, 1.0, 0.0)
    n_double = int(math.log2(L)) - 1

    def body(c, carry):
        for d in range(2):
            cc = c if d == 0 else nc - 1 - c
            tri = tris[d]
            for hh in range(hb):
                head = hg * hb + hh
                qc = q_ref[0, hh, cc]
                kc = k_ref[0, hh, cc]
                vc = v_ref[0, hh, cc]
                gcol = gcol_ref[0, hh, cc]
                grw = grow_ref[0, hh, cc]
                a_neg = -jnp.exp(jnp.zeros((1, 1), F32) + alog_ref[d, head])
                dtb = dtb_ref[d, head]
                g_col = a_neg * _softplus(gcol[:, 2 * d:2 * d + 1] + dtb)
                beta_col = _sigmoid(gcol[:, 2 * d + 1:2 * d + 2])
                g_row = a_neg * _softplus(grw[2 * d:2 * d + 1, :] + dtb)
                G_col = jnp.sum(jnp.where(tri, g_row, 0.0), axis=1, keepdims=True)
                G_row = jnp.sum(jnp.where(tris[1 - d], g_col, 0.0), axis=0, keepdims=True)
                g_tot = jnp.sum(g_row, axis=1, keepdims=True)
                decay = jnp.exp(jnp.where(tri, G_col - G_row, -jnp.inf))
                kcf = kc.astype(F32)
                amat = jnp.where(stricts[d], _dot_nt(kc, kc) * decay * beta_col, 0.0)
                rinv = -amat
                pw = amat
                for _ in range(n_double):
                    pwb = pw.astype(BF16)
                    pw = _dot(pwb, pwb)
                    pwb = pw.astype(BF16)
                    rinv = rinv + pw + _dot(rinv.astype(BF16), pwb)
                resid = -(amat + rinv + _dot3(amat, rinv))
                rinv = rinv + resid + _dot(rinv.astype(BF16), resid.astype(BF16))
                e_g = jnp.exp(G_col)
                rhs = jnp.concatenate([vc.astype(F32) * beta_col, kcf * (beta_col * e_g)], axis=1)
                uw = rhs + _dot(rinv.astype(BF16), rhs.astype(BF16))
                u = uw[:, :HEAD]
                wk = uw[:, HEAD:]
                qk = _dot_nt(qc, kc) * decay
                kd = kcf * jnp.exp(g_tot - G_col)
                s_prev = s_s[d, hh]
                s_b = s_prev.astype(BF16)
                w_new = u - _dot(wk.astype(BF16), s_b)
                o = e_g * _dot(qc, s_b) + _dot(qk.astype(BF16), w_new.astype(BF16))
                s_s[d, hh] = jnp.exp(g_tot) * s_prev + _dot_tn(kd.astype(BF16), w_new.astype(BF16))
                if d == 0:
                    of_s[hh, cc] = o
                else:
                    or_s[hh, cc] = o
        return carry

    lax.fori_loop(0, nc, body, 0)

    def emit(c, carry):
        t0 = pl.multiple_of(c * L, L)
        for hh in range(hb):
            tot = of_s[hh, c] + or_s[hh, c]
            hn = tot * lax.rsqrt(jnp.mean(tot * tot, -1, keepdims=True) + EPS) * ng_ref[...]
            gate = zs_ref[0, pl.ds(t0, L), hh * HEAD:(hh + 1) * HEAD].astype(F32)
            a_ref[0, pl.ds(t0, L), hh * HEAD:(hh + 1) * HEAD] = (hn * gate).astype(BF16)
        return carry

    lax.fori_loop(0, nc, emit, 0)
    if emit_state:
        sn_ref[0] = s_s[...]


def _delta(q, k, v, zs, gates, a_log, dt_bias, norm_g, state, emit_state):
    b, _, t, _ = q.shape
    d = N_HEAD * HEAD
    chunk = DELTA_CHUNK
    nc = t // chunk
    hb = 4
    gcol, grow = _gate_layouts(gates, nc, chunk)
    qkv_spec = pl.BlockSpec((1, hb, nc, chunk, HEAD), lambda i, j: (i, j, 0, 0, 0))
    r5 = lambda a: a.reshape(b, N_HEAD, nc, chunk, HEAD)
    smem = pl.BlockSpec(memory_space=pltpu.SMEM)
    in_specs = [qkv_spec, qkv_spec, qkv_spec,
                pl.BlockSpec((1, t, hb * HEAD), lambda i, j: (i, 0, j)),
                pl.BlockSpec((1, hb, nc, chunk, 4), lambda i, j: (i, j, 0, 0, 0)),
                pl.BlockSpec((1, hb, nc, 4, chunk), lambda i, j: (i, j, 0, 0, 0)),
                smem, smem,
                pl.BlockSpec((1, HEAD), lambda i, j: (0, 0))]
    args = [r5(q), r5(k), r5(v), zs, gcol, grow, a_log, dt_bias, norm_g.reshape(1, HEAD)]
    zero_init = state is None
    s_spec = pl.BlockSpec((1, 2, hb, HEAD, HEAD), lambda i, j: (i, 0, j, 0, 0))
    if not zero_init:
        in_specs.append(s_spec)
        args.append(state)
    out_specs = [pl.BlockSpec((1, t, hb * HEAD), lambda i, j: (i, 0, j))]
    out_shape = [jax.ShapeDtypeStruct((b, t, d), BF16)]
    if emit_state:
        out_specs.append(s_spec)
        out_shape.append(jax.ShapeDtypeStruct((b, 2, N_HEAD, HEAD, HEAD), F32))
    res = pl.pallas_call(
        functools.partial(_delta_kernel, hb=hb, nc=nc, chunk=chunk, zero_init=zero_init, emit_state=emit_state),
        grid=(b, N_HEAD // hb),
        in_specs=in_specs,
        out_specs=out_specs,
        out_shape=out_shape,
        scratch_shapes=[pltpu.VMEM((2, hb, HEAD, HEAD), F32),
                        pltpu.VMEM((hb, nc, chunk, HEAD), F32),
                        pltpu.VMEM((hb, nc, chunk, HEAD), F32)],
        compiler_params=_cparams("arbitrary", "arbitrary"),
        name="delta",
    )(*args)
    return (res[0], res[1]) if emit_state else (res[0], None)


def _out_kernel(a_ref, w_ref, x_ref, mod_ref, ng_ref, wr_ref, x1_ref, h2_ref, lg_ref):
    y = _dot(a_ref[0], w_ref[...])
    x1 = x_ref[0] + mod_ref[0, 2:3, :] * y
    x1_ref[0] = x1
    h2 = _rms_mod(x1, ng_ref[...], mod_ref[0, 4:5, :], mod_ref[0, 3:4, :])
    h2_ref[0] = h2.astype(BF16)
    lg_ref[0] = _dot3(wr_ref[...], h2, dot=_dot_nt)


def _out_proj(a, w_bf, x, mod, norm_g, wr_t):
    b, t, d = x.shape
    tm = ROW_TILE
    bc = mod.shape[0]
    mod_map = (lambda i, j: (i, 0, 0)) if bc > 1 else (lambda i, j: (0, 0, 0))
    tile = pl.BlockSpec((1, tm, d), lambda i, j: (i, j, 0))
    return pl.pallas_call(
        _out_kernel,
        grid=(b, t // tm),
        in_specs=[tile,
                  pl.BlockSpec((d, d), lambda i, j: (0, 0)),
                  tile,
                  pl.BlockSpec((1, 8, d), mod_map),
                  pl.BlockSpec((1, d), lambda i, j: (0, 0)),
                  pl.BlockSpec((N_EXP, d), lambda i, j: (0, 0))],
        out_specs=[tile, tile, pl.BlockSpec((1, N_EXP, tm), lambda i, j: (i, 0, j))],
        out_shape=[jax.ShapeDtypeStruct((b, t, d), F32),
                   jax.ShapeDtypeStruct((b, t, d), BF16),
                   jax.ShapeDtypeStruct((b, N_EXP, t), F32)],
        compiler_params=_cparams("arbitrary", "arbitrary"),
        name="out_proj",
    )(a, w_bf, x, mod, norm_g.reshape(1, d), wr_t)


def _route_kernel(lg_ref, slot_ref, aff_ref, *, cap):
    l = lg_ref[0]
    e, t = l.shape
    mx = jnp.max(l, axis=0, keepdims=True)
    ex = jnp.exp(l - mx)
    aff = ex / jnp.sum(ex, axis=0, keepdims=True)
    aff_ref[0] = aff
    bits = lax.bitcast_convert_type(aff, I32)
    capf = float(cap)

    def count(mask):
        return jnp.sum(jnp.where(mask, 1.0, 0.0), axis=1, keepdims=True)

    thr = jnp.zeros((e, 1), I32)
    for bit in range(30, -1, -1):
        cand = thr | (1 << bit)
        thr = jnp.where(count(bits >= cand) >= capf, cand, thr)
    gt = bits > thr
    eq = bits == thr
    need = capf - count(gt)
    idx = lax.broadcasted_iota(I32, (e, t), 1)
    jm = jnp.zeros((e, 1), I32)
    for bit in range(int(math.log2(t)) - 1, -1, -1):
        cand = jm | (1 << bit)
        jm = jnp.where(count(jnp.logical_and(eq, idx < cand)) < need, cand, jm)
    sel = jnp.logical_or(gt, jnp.logical_and(eq, idx <= jm))
    self_ = jnp.where(sel, 1.0, 0.0)
    blk = 256
    r = lax.broadcasted_iota(I32, (blk, blk), 0)
    c = lax.broadcasted_iota(I32, (blk, blk), 1)
    upper = jnp.where(r < c, 1.0, 0.0).astype(BF16)
    off = jnp.zeros((e, 1), F32)
    pieces = []
    for j in range(t // blk):
        sb = self_[:, j * blk:(j + 1) * blk]
        pieces.append(_dot(sb.astype(BF16), upper) + off)
        off = off + jnp.sum(sb, axis=1, keepdims=True)
    slot = pieces[0] if len(pieces) == 1 else jnp.concatenate(pieces, axis=1)
    slot_ref[0] = jnp.where(sel, slot.astype(I32), -1)


def _route(logits_t, cap):
    b, e, t = logits_t.shape
    spec = pl.BlockSpec((1, e, t), lambda i: (i, 0, 0))
    return pl.pallas_call(
        functools.partial(_route_kernel, cap=cap),
        grid=(b,),
        in_specs=[spec],
        out_specs=[spec, spec],
        out_shape=[jax.ShapeDtypeStruct((b, e, t), I32), jax.ShapeDtypeStruct((b, e, t), F32)],
        compiler_params=_cparams("arbitrary"),
        name="route",
    )(logits_t)


def _gather_kernel(h_ref, slot_ref, aff_ref, xs_ref, gs_ref, *, cap):
    t = h_ref.shape[1]
    rows = lax.broadcasted_iota(I32, (cap, t), 0)

    def body(e, carry):
        hit = rows == slot_ref[0, e]
        p = jnp.where(hit, 1.0, 0.0).astype(BF16)
        xs_ref[e, 0] = _dot(p, h_ref[0]).astype(BF16)
        gs_ref[e, 0] = jnp.sum(jnp.where(hit, aff_ref[0, e], 0.0), axis=1, keepdims=True)
        return carry

    lax.fori_loop(0, N_EXP, body, 0)


def _gather(h2, slotm, aff_t, cap):
    b, t, d = h2.shape
    e = N_EXP
    row_spec = pl.BlockSpec((1, e, 1, t), lambda i: (i, 0, 0, 0))
    return pl.pallas_call(
        functools.partial(_gather_kernel, cap=cap),
        grid=(b,),
        in_specs=[pl.BlockSpec((1, t, d), lambda i: (i, 0, 0)), row_spec, row_spec],
        out_specs=[pl.BlockSpec((e, 1, cap, d), lambda i: (0, i, 0, 0)),
                   pl.BlockSpec((e, 1, cap, 1), lambda i: (0, i, 0, 0))],
        out_shape=[jax.ShapeDtypeStruct((e, b, cap, d), BF16),
                   jax.ShapeDtypeStruct((e, b, cap, 1), F32)],
        compiler_params=_cparams("arbitrary"),
        name="gather",
    )(h2, slotm.reshape(b, e, 1, t), aff_t.reshape(b, e, 1, t))


def _ffn_kernel(xc_ref, gc_ref, xs_ref, gs_ref, wg_ref, wu_ref, wd_ref, yc_ref, ys_ref):
    wg = wg_ref[0, 0].astype(BF16)
    wu = wu_ref[0, 0].astype(BF16)
    wd = wd_ref[0, 0].astype(BF16)
    for x_ref, g_ref, y_ref in ((xc_ref, gc_ref, yc_ref), (xs_ref, gs_ref, ys_ref)):
        rows = x_ref.shape[1]
        rc = min(256, rows)
        for r0 in range(0, rows, rc):
            x = x_ref[0, r0:r0 + rc]
            hdn = (_silu(_dot(x, wg)) * _dot(x, wu)).astype(BF16)
            y_ref[0, r0:r0 + rc] = (_dot(hdn, wd) * g_ref[0, r0:r0 + rc]).astype(BF16)


def _ffn(xc, gc, xs, gs, w_gate, w_up, w_down, layer):
    e, rc, d = xc.shape
    rs = xs.shape[1]
    f = w_gate.shape[-1]
    wmap = lambda i: (layer, i, 0, 0)
    return pl.pallas_call(
        _ffn_kernel,
        grid=(e,),
        in_specs=[pl.BlockSpec((1, rc, d), lambda i: (i, 0, 0)),
                  pl.BlockSpec((1, rc, 1), lambda i: (i, 0, 0)),
                  pl.BlockSpec((1, rs, d), lambda i: (i, 0, 0)),
                  pl.BlockSpec((1, rs, 1), lambda i: (i, 0, 0)),
                  pl.BlockSpec((1, 1, d, f), wmap),
                  pl.BlockSpec((1, 1, d, f), wmap),
                  pl.BlockSpec((1, 1, f, d), wmap)],
        out_specs=[pl.BlockSpec((1, rc, d), lambda i: (i, 0, 0)),
                   pl.BlockSpec((1, rs, d), lambda i: (i, 0, 0))],
        out_shape=[jax.ShapeDtypeStruct((e, rc, d), BF16),
                   jax.ShapeDtypeStruct((e, rs, d), BF16)],
        compiler_params=_cparams("arbitrary"),
        name="expert_ffn",
    )(xc, gc, xs, gs, w_gate, w_up, w_down)


def _scatter_kernel(y_ref, slot_ref, x1_ref, mod_ref, fg_ref, o_ref, *, cap, final):
    tq = x1_ref.shape[1]
    e = N_EXP
    slot = slot_ref[0]
    if e * cap <= 1024:
        lanes = lax.broadcasted_iota(I32, (tq, e * cap), 1)
        hit = None
        for ee in range(e):
            gs = jnp.where(slot[:, ee:ee + 1] >= 0, slot[:, ee:ee + 1] + ee * cap, -1)
            m = lanes == gs
            hit = m if hit is None else jnp.logical_or(hit, m)
        pt = jnp.where(hit, 1.0, 0.0).astype(BF16)
        acc = _dot(pt, y_ref[:, 0].reshape(e * cap, -1))
    else:
        lanes = lax.broadcasted_iota(I32, (tq, cap), 1)
        acc = None
        for ee in range(e):
            pt = jnp.where(lanes == slot[:, ee:ee + 1], 1.0, 0.0).astype(BF16)
            part = _dot(pt, y_ref[ee, 0])
            acc = part if acc is None else acc + part
    x2 = x1_ref[0] + mod_ref[0, 5:6, :] * acc
    if final:
        x2 = x2 * lax.rsqrt(jnp.mean(x2 * x2, -1, keepdims=True) + EPS) * fg_ref[...]
    o_ref[0] = x2


def _scatter(y, slot_col, x1, mod, final_g, cap, final):
    b, t, d = x1.shape
    e = N_EXP
    tq = min(t, 512)
    bc = mod.shape[0]
    mod_map = (lambda i, j: (i, 0, 0)) if bc > 1 else (lambda i, j: (0, 0, 0))
    return pl.pallas_call(
        functools.partial(_scatter_kernel, cap=cap, final=final),
        grid=(b, t // tq),
        in_specs=[pl.BlockSpec((e, 1, cap, d), lambda i, j: (0, i, 0, 0)),
                  pl.BlockSpec((1, tq, e), lambda i, j: (i, j, 0)),
                  pl.BlockSpec((1, tq, d), lambda i, j: (i, j, 0)),
                  pl.BlockSpec((1, 8, d), mod_map),
                  pl.BlockSpec((1, d), lambda i, j: (0, 0))],
        out_specs=pl.BlockSpec((1, tq, d), lambda i, j: (i, j, 0)),
        out_shape=jax.ShapeDtypeStruct((b, t, d), F32),
        compiler_params=_cparams("arbitrary", "arbitrary"),
        name="scatter",
    )(y, slot_col, x1, mod, final_g.reshape(1, d))


def kernel(x_prompt, x_sample, state_mlstm_C, state_mlstm_n, state_mlstm_m, state_delta_S, c, c_ctx, norm1_g, norm2_g, w_ada, b_ada, w_in, w_out, w_router, w_gate, w_up, w_down, final_g, mlstm_ig_b, mlstm_fg_b, mlstm_norm_g, delta_conv_w, delta_A_log, delta_dt_bias, delta_norm_g):
    depth = w_in.shape[0]
    d = x_prompt.shape[-1]
    nb_s = x_sample.shape[0]
    cond8 = jnp.zeros((8, d), F32).at[0].set(c_ctx).at[1:1 + nb_s].set(c)
    mods = _modulation(cond8, w_ada, b_ada).reshape(depth, 8, 6, d)
    mods = jnp.pad(mods, ((0, 0), (0, 0), (0, 2), (0, 0)))
    xs = [x_prompt, x_sample]
    conv_rows = [x_prompt.shape[1], GRID_W]
    new_states = {}
    for i in range(depth):
        mod = [mods[i, 0:1], mods[i, 1:1 + nb_s]]
        w_qkvo = w_in[i, :, :4 * d].astype(BF16)
        w_g = w_in[i, :, 4 * d:]
        w_o = w_out[i].astype(BF16)
        wr_t = w_router[i].T
        j = i // 2
        routed = []
        for p in range(2):
            x = xs[p]
            b, t, _ = x.shape
            cap = CAP_FACTOR * t // N_EXP
            if i % 2 == 0:
                q, k, v, og, gates = _in_proj(x, mod[p], norm1_g[i], w_qkvo, w_g, None, "mlstm", None)
                state = None if p == 0 else (state_mlstm_C[:, j], state_mlstm_n[:, j], state_mlstm_m[:, j])
                a, st = _mlstm(q, k, v, og, gates, mlstm_ig_b[j], mlstm_fg_b[j], mlstm_norm_g[j], state, p == 0)
                if p == 0:
                    new_states.setdefault("m", []).append(st)
            else:
                q, k, v, og, gates = _in_proj(x, mod[p], norm1_g[i], w_qkvo, w_g, delta_conv_w[j], "delta", conv_rows[p])
                state = None if p == 0 else state_delta_S[:, j]
                a, st = _delta(q, k, v, og, gates, delta_A_log[j], delta_dt_bias[j], delta_norm_g[j], state, p == 0)
                if p == 0:
                    new_states.setdefault("d", []).append(st)
            x1, h2, logits_t = _out_proj(a, w_o, x, mod[p], norm2_g[i], wr_t)
            slotm, aff_t = _route(logits_t, cap)
            xg, gsel = _gather(h2, slotm, aff_t, cap)
            routed.append((x1, slotm, xg, gsel, cap, b))
        (x1c, slc, xgc, gsc, capc, bc_), (x1s, sls, xgs, gss, caps, bs_) = routed
        yc, ys = _ffn(xgc.reshape(N_EXP, bc_ * capc, d), gsc.reshape(N_EXP, bc_ * capc, 1),
                      xgs.reshape(N_EXP, bs_ * caps, d), gss.reshape(N_EXP, bs_ * caps, 1),
                      w_gate, w_up, w_down, i)
        final = i == depth - 1
        xs = [_scatter(yc.reshape(N_EXP, bc_, capc, d), slc.transpose(0, 2, 1), x1c, mod[0], final_g, capc, final),
              _scatter(ys.reshape(N_EXP, bs_, caps, d), sls.transpose(0, 2, 1), x1s, mod[1], final_g, caps, final)]
    ms = new_states["m"]
    new_c = jnp.stack([s[0] for s in ms], 1)
    new_n = jnp.stack([s[1] for s in ms], 1)
    new_m = jnp.stack([s[2] for s in ms], 1)
    new_s = jnp.stack(new_states["d"], 1)
    return (xs[0], xs[1], new_c, new_n, new_m, new_s)
```

```python
import functools
import math

import jax
import jax.numpy as jnp
from jax import lax
from jax.experimental import pallas as pl
from jax.experimental.pallas import tpu as pltpu

F32 = jnp.float32
BF16 = jnp.bfloat16
I32 = jnp.int32

N_HEAD = 8
HEAD = 128
N_EXP = 16
CAP_FACTOR = 2
GRID_W = 64
CONV_W = 5
DELTA_CHUNK = 64
MLSTM_CHUNK = 128
EPS = 1e-6
ROW_TILE = 256
VMEM_LIMIT = 56 * 1024 * 1024


def _cparams(*sem):
    return pltpu.CompilerParams(dimension_semantics=sem, vmem_limit_bytes=VMEM_LIMIT)


def _dot(a, b):
    return jnp.dot(a, b, preferred_element_type=F32)


def _dot_nt(a, b):
    return lax.dot_general(a, b, (((1,), (1,)), ((), ())), preferred_element_type=F32)


def _dot_tn(a, b):
    return lax.dot_general(a, b, (((0,), (0,)), ((), ())), preferred_element_type=F32)


def _split(x):
    hi = x.astype(BF16)
    lo = (x - hi.astype(F32)).astype(BF16)
    return hi, lo


def _dot3(a, b, dot=_dot):
    ah, al = _split(a)
    bh, bl = _split(b)
    return dot(ah, bh) + (dot(ah, bl) + dot(al, bh))


def _sigmoid(x):
    return 1.0 / (1.0 + jnp.exp(-x))


def _silu(x):
    return x * _sigmoid(x)


def _softplus(x):
    return jnp.maximum(x, 0.0) + jnp.log1p(jnp.exp(-jnp.abs(x)))


def _log_sigmoid(x):
    return -_softplus(-x)


def _rms_mod(x, g, scale, shift):
    y = x * lax.rsqrt(jnp.mean(x * x, -1, keepdims=True) + EPS)
    return (y * g) * (1.0 + scale) + shift


def _mod_kernel(cond_ref, w_ref, b_ref, o_ref):
    c = cond_ref[...]
    o_ref[0] = _dot3(_silu(c), w_ref[0]) + b_ref[0]


def _modulation(cond8, w_ada, b_ada):
    depth, d, n6 = w_ada.shape
    tn = 1536
    return pl.pallas_call(
        _mod_kernel,
        grid=(depth, n6 // tn),
        in_specs=[pl.BlockSpec((8, d), lambda l, j: (0, 0)),
                  pl.BlockSpec((1, d, tn), lambda l, j: (l, 0, j)),
                  pl.BlockSpec((1, 1, tn), lambda l, j: (l, 0, j))],
        out_specs=pl.BlockSpec((1, 8, tn), lambda l, j: (l, 0, j)),
        out_shape=jax.ShapeDtypeStruct((depth, 8, n6), F32),
        compiler_params=_cparams("arbitrary", "arbitrary"),
        name="modulation",
    )(cond8, w_ada, b_ada.reshape(depth, 1, n6))


def _in_kernel(x_ref, mod_ref, ng_ref, w_ref, wg_ref, *rest, kind, conv_row):
    if kind == "delta":
        cw_ref, q_ref, k_ref, v_ref, og_ref, gates_ref = rest
    else:
        q_ref, k_ref, v_ref, og_ref, gates_ref = rest
    x = x_ref[0]
    tm, d = x.shape
    h = _rms_mod(x, ng_ref[...], mod_ref[0, 1:2, :], mod_ref[0, 0:1, :])
    hb = h.astype(BF16)
    gates_ref[0] = _dot3(h, wg_ref[...])
    if kind == "delta":
        pos = lax.broadcasted_iota(I32, (tm, 1), 0) % conv_row
    for j in range(4):
        acc = _dot(hb, w_ref[:, j * d:(j + 1) * d])
        if kind == "mlstm":
            if j == 1:
                acc = acc * (HEAD ** -0.5)
            if j == 3:
                og_ref[0] = _sigmoid(acc).astype(BF16)
                continue
        else:
            if j == 3:
                og_ref[0] = _silu(acc).astype(BF16)
                continue
            cw = cw_ref[:, j * d:(j + 1) * d]
            out = acc * cw[2:3, :]
            for tap, sft in ((0, -2), (1, -1), (3, 1), (4, 2)):
                shifted = pltpu.roll(acc, (-sft) % tm, 0)
                valid = jnp.logical_and(pos + sft >= 0, pos + sft < conv_row)
                out = out + jnp.where(valid, shifted, 0.0) * cw[tap:tap + 1, :]
            acc = _silu(out)
        dst = (q_ref, k_ref, v_ref)[j]
        for hh in range(N_HEAD):
            blk = acc[:, hh * HEAD:(hh + 1) * HEAD]
            if kind == "delta" and j < 2:
                blk = blk * lax.rsqrt(jnp.sum(blk * blk, -1, keepdims=True) + EPS)
                if j == 0:
                    blk = blk * (HEAD ** -0.5)
            dst[0, hh] = blk.astype(BF16)


def _in_proj(x, mod, norm_g, w_bf, w_gate, conv_w, kind, conv_row):
    b, t, d = x.shape
    tm = ROW_TILE
    nt = t // tm
    bc = mod.shape[0]
    mod_map = (lambda i, j: (i, 0, 0)) if bc > 1 else (lambda i, j: (0, 0, 0))
    in_specs = [pl.BlockSpec((1, tm, d), lambda i, j: (i, j, 0)),
                pl.BlockSpec((1, 8, d), mod_map),
                pl.BlockSpec((1, d), lambda i, j: (0, 0)),
                pl.BlockSpec((d, 4 * d), lambda i, j: (0, 0)),
                pl.BlockSpec((d, 32), lambda i, j: (0, 0))]
    args = [x, mod, norm_g.reshape(1, d), w_bf, w_gate]
    if kind == "delta":
        in_specs.append(pl.BlockSpec((CONV_W, 3 * d), lambda i, j: (0, 0)))
        args.append(conv_w)
    hd_spec = pl.BlockSpec((1, N_HEAD, tm, HEAD), lambda i, j: (i, 0, j, 0))
    hd_shape = jax.ShapeDtypeStruct((b, N_HEAD, t, HEAD), BF16)
    return pl.pallas_call(
        functools.partial(_in_kernel, kind=kind, conv_row=conv_row),
        grid=(b, nt),
        in_specs=in_specs,
        out_specs=[hd_spec, hd_spec, hd_spec,
                   pl.BlockSpec((1, tm, d), lambda i, j: (i, j, 0)),
                   pl.BlockSpec((1, tm, 32), lambda i, j: (i, j, 0))],
        out_shape=[hd_shape, hd_shape, hd_shape,
                   jax.ShapeDtypeStruct((b, t, d), BF16),
                   jax.ShapeDtypeStruct((b, t, 32), F32)],
        compiler_params=_cparams("arbitrary", "arbitrary"),
        name="in_proj_" + kind,
    )(*args)


def _gate_layouts(gates, nc, chunk):
    b, t, _ = gates.shape
    g = gates.reshape(b, nc, chunk, 4, N_HEAD)
    return g.transpose(0, 4, 1, 2, 3), g.transpose(0, 4, 1, 3, 2)


def _mlstm_kernel(*refs, hb, nc, chunk, zero_init, emit_state):
    it = iter(refs)
    q_ref, k_ref, v_ref, og_ref, gcol_ref, grow_ref, igb_ref, fgb_ref, ng_ref = (next(it) for _ in range(9))
    if not zero_init:
        c0_ref, n0_ref, m0_ref = (next(it) for _ in range(3))
    a_ref = next(it)
    if emit_state:
        cn_ref, nn_ref, mn_ref = (next(it) for _ in range(3))
    c_s, n_s, m_s, hf_s, hr_s = (next(it) for _ in range(5))
    L = chunk
    hg = pl.program_id(1)
    if zero_init:
        c_s[...] = jnp.zeros_like(c_s)
        n_s[...] = jnp.zeros_like(n_s)
        m_s[...] = jnp.zeros_like(m_s)
    else:
        c_s[...] = c0_ref[0]
        n_s[...] = n0_ref[0]
        m_s[...] = m0_ref[0]
    row = lax.broadcasted_iota(I32, (L, L), 0)
    col = lax.broadcasted_iota(I32, (L, L), 1)
    tris = (row >= col, col >= row)

    def body(c, carry):
        chains = [(d, hh) for d in range(2) for hh in range(hb)]
        st = []
        for d, hh in chains:
            cc = c if d == 0 else nc - 1 - c
            tri = tris[d]
            head = hg * hb + hh
            qc = q_ref[0, hh, cc]
            kc = k_ref[0, hh, cc]
            vc = v_ref[0, hh, cc]
            gcol = gcol_ref[0, hh, cc]
            grw = grow_ref[0, hh, cc]
            igb = igb_ref[d, head]
            fgb = fgb_ref[d, head]
            i_col = gcol[:, 2 * d:2 * d + 1] + igb
            f_col = _log_sigmoid(gcol[:, 2 * d + 1:2 * d + 2] + fgb)
            i_row = grw[2 * d:2 * d + 1, :] + igb
            f_row = _log_sigmoid(grw[2 * d + 1:2 * d + 2, :] + fgb)
            b_col = jnp.sum(jnp.where(tri, f_row, 0.0), axis=1, keepdims=True)
            b_row = jnp.sum(jnp.where(tris[1 - d], f_col, 0.0), axis=0, keepdims=True)
            bl = jnp.sum(f_row, axis=1, keepdims=True)
            m_prev = m_s[d, hh]
            c_prev = c_s[d, hh]
            n_prev = n_s[d, hh]
            logd = jnp.where(tri, b_col - b_row + i_row, -jnp.inf)
            inter = b_col + m_prev
            mt = jnp.maximum(inter, jnp.max(logd, axis=1, keepdims=True))
            logw = bl - b_col + i_col
            m_new = jnp.maximum(bl + m_prev, jnp.max(logw, axis=0, keepdims=True))
            kw = kc.astype(F32) * jnp.exp(logw - m_new)
            st.append(dict(cc=cc, qc=qc, kc=kc, vc=vc, logd=logd, inter=inter, mt=mt, m_new=m_new, kw=kw,
                           dec=jnp.exp(bl + m_prev - m_new), c_prev=c_prev, n_prev=n_prev))
        for z in st:
            z["qk"] = _dot_nt(z["qc"], z["kc"])
            z["qc_c"] = _dot(z["qc"], z["c_prev"].astype(BF16))
        for z in st:
            z["s"] = z["qk"] * jnp.exp(z["logd"] - z["mt"])
            z["sv"] = _dot(z["s"].astype(BF16), z["vc"])
        for z in st:
            z["kv"] = _dot_tn(z["kw"].astype(BF16), z["vc"])
        for (d, hh), z in zip(chains, st):
            sc = jnp.exp(z["inter"] - z["mt"])
            num = z["sv"] + sc * z["qc_c"]
            den = (jnp.sum(z["s"], axis=1, keepdims=True)
                   + sc * jnp.sum(z["qc"].astype(F32) * z["n_prev"], axis=1, keepdims=True))
            hout = num / jnp.maximum(jnp.abs(den), jnp.exp(-z["mt"]))
            if d == 0:
                hf_s[hh, z["cc"]] = hout
            else:
                hr_s[hh, z["cc"]] = hout
            c_s[d, hh] = z["dec"] * z["c_prev"] + z["kv"]
            n_s[d, hh] = z["dec"] * z["n_prev"] + jnp.sum(z["kw"], axis=0, keepdims=True)
            m_s[d, hh] = z["m_new"]
        return carry

    lax.fori_loop(0, nc, body, 0)

    def emit(c, carry):
        t0 = pl.multiple_of(c * L, L)
        for hh in range(hb):
            tot = hf_s[hh, c] + hr_s[hh, c]
            hn = tot * lax.rsqrt(jnp.mean(tot * tot, -1, keepdims=True) + EPS) * ng_ref[:, hh * HEAD:(hh + 1) * HEAD]
            gate = og_ref[0, pl.ds(t0, L), hh * HEAD:(hh + 1) * HEAD].astype(F32)
            a_ref[0, pl.ds(t0, L), hh * HEAD:(hh + 1) * HEAD] = (gate * hn).astype(BF16)
        return carry

    lax.fori_loop(0, nc, emit, 0)
    if emit_state:
        cn_ref[0] = c_s[...]
        nn_ref[0] = n_s[...]
        mn_ref[0] = m_s[...]


def _mlstm(q, k, v, og, gates, ig_b, fg_b, norm_g, state, emit_state):
    b, _, t, _ = q.shape
    d = N_HEAD * HEAD
    chunk = min(t, MLSTM_CHUNK)
    nc = t // chunk
    hb = 4
    gcol, grow = _gate_layouts(gates, nc, chunk)
    qkv_spec = pl.BlockSpec((1, hb, nc, chunk, HEAD), lambda i, j: (i, j, 0, 0, 0))
    r5 = lambda a: a.reshape(b, N_HEAD, nc, chunk, HEAD)
    smem = pl.BlockSpec(memory_space=pltpu.SMEM)
    in_specs = [qkv_spec, qkv_spec, qkv_spec,
                pl.BlockSpec((1, t, hb * HEAD), lambda i, j: (i, 0, j)),
                pl.BlockSpec((1, hb, nc, chunk, 4), lambda i, j: (i, j, 0, 0, 0)),
                pl.BlockSpec((1, hb, nc, 4, chunk), lambda i, j: (i, j, 0, 0, 0)),
                smem, smem,
                pl.BlockSpec((1, hb * HEAD), lambda i, j: (0, j))]
    args = [r5(q), r5(k), r5(v), og, gcol, grow, ig_b, fg_b, norm_g.reshape(1, d)]
    zero_init = state is None
    c_spec = pl.BlockSpec((1, 2, hb, HEAD, HEAD), lambda i, j: (i, 0, j, 0, 0))
    n_spec = pl.BlockSpec((1, 2, hb, 1, HEAD), lambda i, j: (i, 0, j, 0, 0))
    m_spec = pl.BlockSpec((1, 2, hb, 1, 1), lambda i, j: (i, 0, j, 0, 0))
    if not zero_init:
        c0, n0, m0 = state
        in_specs += [c_spec, n_spec, m_spec]
        args += [c0, n0.reshape(b, 2, N_HEAD, 1, HEAD), m0.reshape(b, 2, N_HEAD, 1, 1)]
    out_specs = [pl.BlockSpec((1, t, hb * HEAD), lambda i, j: (i, 0, j))]
    out_shape = [jax.ShapeDtypeStruct((b, t, d), BF16)]
    if emit_state:
        out_specs += [c_spec, n_spec, m_spec]
        out_shape += [jax.ShapeDtypeStruct((b, 2, N_HEAD, HEAD, HEAD), F32),
                      jax.ShapeDtypeStruct((b, 2, N_HEAD, 1, HEAD), F32),
                      jax.ShapeDtypeStruct((b, 2, N_HEAD, 1, 1), F32)]
    res = pl.pallas_call(
        functools.partial(_mlstm_kernel, hb=hb, nc=nc, chunk=chunk, zero_init=zero_init, emit_state=emit_state),
        grid=(b, N_HEAD // hb),
        in_specs=in_specs,
        out_specs=out_specs,
        out_shape=out_shape,
        scratch_shapes=[pltpu.VMEM((2, hb, HEAD, HEAD), F32),
                        pltpu.VMEM((2, hb, 1, HEAD), F32),
                        pltpu.VMEM((2, hb, 1, 1), F32),
                        pltpu.VMEM((hb, nc, chunk, HEAD), F32),
                        pltpu.VMEM((hb, nc, chunk, HEAD), F32)],
        compiler_params=_cparams("arbitrary", "arbitrary"),
        name="mlstm",
    )(*args)
    if emit_state:
        a, cn, nn, mn = res
        return a, (cn, nn.reshape(b, 2, N_HEAD, HEAD), mn.reshape(b, 2, N_HEAD))
    return res[0], None


def _delta_kernel(*refs, hb, nc, chunk, zero_init, emit_state):
    it = iter(refs)
    q_ref, k_ref, v_ref, zs_ref, grow_ref, alog_ref, dtb_ref, ng_ref = (next(it) for _ in range(8))
    if not zero_init:
        s0_ref = next(it)
    a_ref = next(it)
    if emit_state:
        sn_ref = next(it)
    s_s, o_s, u_s, wk_s, qk_s, kdf_s, kdb_s = (next(it) for _ in range(7))
    L = chunk
    hg = pl.program_id(1)
    for hh in range(hb):
        if zero_init:
            s_s[hh] = jnp.zeros((HEAD, 2 * HEAD), F32)
        else:
            s_s[hh] = jnp.concatenate([s0_ref[0, 0, hh], s0_ref[0, 1, hh]], axis=1)
    o_s[...] = jnp.zeros_like(o_s)
    t_i = lax.broadcasted_iota(I32, (L, 2 * L), 0)
    lane = lax.broadcasted_iota(I32, (L, 2 * L), 1)
    lo = lane < L
    s_i = jnp.where(lo, lane, lane - L)
    lo_row = lax.broadcasted_iota(I32, (1, 2 * L), 1) < L
    hi = jnp.logical_not(lo)

    def bsel(a, b):
        return jnp.logical_or(jnp.logical_and(lo, a), jnp.logical_and(hi, b))

    incl = bsel(s_i <= t_i, s_i >= t_i)
    strict = bsel(s_i < t_i, s_i > t_i)
    eye2 = s_i == t_i
    n_lvl = int(math.log2(L))
    lvls = []
    for kbit in range(n_lvl):
        same = (t_i >> (kbit + 1)) == (s_i >> (kbit + 1))
        tb = ((t_i >> kbit) & 1) == 1
        sb = ((s_i >> kbit) & 1) == 1
        fwd_m = jnp.logical_and(tb, jnp.logical_not(sb))
        bwd_m = jnp.logical_and(sb, jnp.logical_not(tb))
        lvls.append(jnp.logical_and(same, bsel(fwd_m, bwd_m)))
    zero_l = jnp.zeros((L, HEAD), F32)

    def half_sums(z):
        return (jnp.sum(jnp.where(lo, z, 0.0), axis=1, keepdims=True),
                jnp.sum(jnp.where(lo, 0.0, z), axis=1, keepdims=True))

    def gate_cum(hh, cc):
        head = hg * hb + hh
        rows = grow_ref[0, hh, cc]
        a2 = -jnp.exp(jnp.where(lo_row, alog_ref[0, head], alog_ref[1, head]))
        dtb2 = jnp.where(lo_row, dtb_ref[0, head], dtb_ref[1, head])
        g_row = a2 * _softplus(rows[0:1, :] + dtb2)
        gc_f, gc_b = half_sums(jnp.where(incl, g_row, 0.0))
        gt_f = jnp.sum(jnp.where(lo_row, g_row, 0.0), axis=1, keepdims=True)
        gt_b = jnp.sum(jnp.where(lo_row, 0.0, g_row), axis=1, keepdims=True)
        return rows, gc_f, gc_b, gt_f, gt_b

    def pair_dot(p, q):
        bd = jnp.concatenate([jnp.where(lo, q, 0.0), jnp.where(lo, 0.0, q)], axis=0).astype(BF16)
        return _dot(p.astype(BF16), bd)

    def local_body(c, carry):
        chains = [(hh, 2 * c + u) for u in range(2) for hh in range(hb)]
        st = []
        for hh, cc in chains:
            qc = q_ref[0, hh, cc]
            kc = k_ref[0, hh, cc]
            rows, gc_f, gc_b, gt_f, gt_b = gate_cum(hh, cc)
            gc2 = jnp.where(lo, gc_f, gc_b)
            gr2 = jnp.sum(jnp.where(eye2, gc2, 0.0), axis=0, keepdims=True)
            beta_f, beta_b = half_sums(jnp.where(eye2, _sigmoid(rows[1:2, :]), 0.0))
            decay = jnp.exp(jnp.where(incl, gc2 - gr2, -jnp.inf))
            big = _dot_nt(jnp.concatenate([qc, kc], axis=0), jnp.concatenate([kc, kc], axis=0))
            st.append(dict(hh=hh, cc=cc, kc=kc, gc_f=gc_f, gc_b=gc_b, gt_f=gt_f, gt_b=gt_b,
                           beta_f=beta_f, beta_b=beta_b, decay=decay, big=big))
        for z in st:
            qk_s[z["hh"], z["cc"]] = z["big"][:L] * z["decay"]
            z["amat"] = jnp.where(strict, z["big"][L:] * z["decay"] * jnp.where(lo, z["beta_f"], z["beta_b"]), 0.0)
            z["rinv"] = -jnp.where(lvls[0], z["amat"], 0.0)
        for kbit in range(1, n_lvl):
            for z in st:
                lk = jnp.where(lvls[kbit], z["amat"], 0.0)
                z["x"] = lk + pair_dot(z["rinv"], lk)
            for z in st:
                z["rinv"] = z["rinv"] - (z["x"] + pair_dot(z["x"], z["rinv"]))
        zero2 = jnp.zeros((L, 2 * HEAD), F32)
        for z in st:
            kcf = z["kc"].astype(F32)
            vcf = v_ref[0, z["hh"], z["cc"]].astype(F32)
            bf_, bb_ = z["beta_f"], z["beta_b"]
            z["rhs_f"] = jnp.concatenate([vcf * bf_, kcf * (bf_ * jnp.exp(z["gc_f"]))], axis=1)
            z["rhs_b"] = jnp.concatenate([vcf * bb_, kcf * (bb_ * jnp.exp(z["gc_b"]))], axis=1)
            wbd = jnp.concatenate([jnp.concatenate([z["rhs_f"], zero2], axis=1),
                                   jnp.concatenate([zero2, z["rhs_b"]], axis=1)], axis=0).astype(BF16)
            z["uw"] = _dot(z["rinv"].astype(BF16), wbd)
            kdf_s[z["hh"], z["cc"]] = (kcf * jnp.exp(z["gt_f"] - z["gc_f"])).astype(BF16)
            kdb_s[z["hh"], z["cc"]] = (kcf * jnp.exp(z["gt_b"] - z["gc_b"])).astype(BF16)
        for z in st:
            uw, rhs_f, rhs_b = z["uw"], z["rhs_f"], z["rhs_b"]
            u_s[z["hh"], z["cc"]] = jnp.concatenate([rhs_f[:, :HEAD] + uw[:, :HEAD],
                                                    rhs_b[:, :HEAD] + uw[:, 2 * HEAD:3 * HEAD]], axis=1)
            wk_s[z["hh"], z["cc"]] = jnp.concatenate([rhs_f[:, HEAD:] + uw[:, HEAD:2 * HEAD],
                                                     rhs_b[:, HEAD:] + uw[:, 3 * HEAD:]], axis=1).astype(BF16)
        return carry

    lax.fori_loop(0, nc // 2, local_body, 0)

    lane2 = lax.broadcasted_iota(I32, (1, 2 * HEAD), 1)

    def scan_body(c, carry):
        cr = nc - 1 - c
        st = []
        for hh in range(hb):
            _, gc_f, _, gt_f, _ = gate_cum(hh, c)
            _, _, gc_b, _, gt_b = gate_cum(hh, cr)
            s2 = s_s[hh]
            sb = s2.astype(BF16)
            ws_f = _dot(jnp.concatenate([wk_s[hh, c][:, :HEAD], q_ref[0, hh, c]], axis=0), sb[:, :HEAD])
            ws_b = _dot(jnp.concatenate([wk_s[hh, cr][:, HEAD:], q_ref[0, hh, cr]], axis=0), sb[:, HEAD:])
            st.append(dict(hh=hh, gc_f=gc_f, gc_b=gc_b, gt_f=gt_f, gt_b=gt_b, s2=s2, ws_f=ws_f, ws_b=ws_b))
        for z in st:
            hh = z["hh"]
            w_f = u_s[hh, c][:, :HEAD] - z["ws_f"][:L]
            w_b = u_s[hh, cr][:, HEAD:] - z["ws_b"][:L]
            z["wbd"] = jnp.concatenate([jnp.concatenate([w_f, zero_l], axis=1),
                                        jnp.concatenate([zero_l, w_b], axis=1)], axis=0).astype(BF16)
            qkp = jnp.where(lo, qk_s[hh, c], qk_s[hh, cr]).astype(BF16)
            z["ow"] = _dot(qkp, z["wbd"])
        for z in st:
            hh = z["hh"]
            kd2 = jnp.concatenate([kdf_s[hh, c], kdb_s[hh, cr]], axis=0)
            egl = jnp.exp(jnp.where(lane2 < HEAD, z["gt_f"], z["gt_b"]))
            s_s[hh] = egl * z["s2"] + _dot_tn(kd2, z["wbd"])
        for z in st:
            hh = z["hh"]
            o_s[hh, c] += jnp.exp(z["gc_f"]) * z["ws_f"][L:] + z["ow"][:, :HEAD]
            o_s[hh, cr] += jnp.exp(z["gc_b"]) * z["ws_b"][L:] + z["ow"][:, HEAD:]
        return carry

    lax.fori_loop(0, nc, scan_body, 0)

    def emit(c, carry):
        t0 = pl.multiple_of(c * L, L)
        for hh in range(hb):
            tot = o_s[hh, c]
            hn = tot * lax.rsqrt(jnp.mean(tot * tot, -1, keepdims=True) + EPS) * ng_ref[...]
            gate = zs_ref[0, pl.ds(t0, L), hh * HEAD:(hh + 1) * HEAD].astype(F32)
            a_ref[0, pl.ds(t0, L), hh * HEAD:(hh + 1) * HEAD] = (hn * gate).astype(BF16)
        return carry

    lax.fori_loop(0, nc, emit, 0)
    if emit_state:
        for hh in range(hb):
            sn_ref[0, 0, hh] = s_s[hh][:, :HEAD]
            sn_ref[0, 1, hh] = s_s[hh][:, HEAD:]


def _delta(q, k, v, zs, gates, a_log, dt_bias, norm_g, state, emit_state):
    b, _, t, _ = q.shape
    d = N_HEAD * HEAD
    chunk = DELTA_CHUNK
    nc = t // chunk
    hb = 4
    assert 2 * chunk == HEAD and nc % 2 == 0
    grow = gates.reshape(b, nc, chunk, 2, 2, N_HEAD).transpose(0, 5, 1, 4, 3, 2).reshape(b, N_HEAD, nc, 2, 2 * chunk)
    qkv_spec = pl.BlockSpec((1, hb, nc, chunk, HEAD), lambda i, j: (i, j, 0, 0, 0))
    r5 = lambda a: a.reshape(b, N_HEAD, nc, chunk, HEAD)
    smem = pl.BlockSpec(memory_space=pltpu.SMEM)
    in_specs = [qkv_spec, qkv_spec, qkv_spec,
                pl.BlockSpec((1, t, hb * HEAD), lambda i, j: (i, 0, j)),
                pl.BlockSpec((1, hb, nc, 2, 2 * chunk), lambda i, j: (i, j, 0, 0, 0)),
                smem, smem,
                pl.BlockSpec((1, HEAD), lambda i, j: (0, 0))]
    args = [r5(q), r5(k), r5(v), zs, grow, a_log, dt_bias, norm_g.reshape(1, HEAD)]
    zero_init = state is None
    s_spec = pl.BlockSpec((1, 2, hb, HEAD, HEAD), lambda i, j: (i, 0, j, 0, 0))
    if not zero_init:
        in_specs.append(s_spec)
        args.append(state)
    out_specs = [pl.BlockSpec((1, t, hb * HEAD), lambda i, j: (i, 0, j))]
    out_shape = [jax.ShapeDtypeStruct((b, t, d), BF16)]
    if emit_state:
        out_specs.append(s_spec)
        out_shape.append(jax.ShapeDtypeStruct((b, 2, N_HEAD, HEAD, HEAD), F32))
    res = pl.pallas_call(
        functools.partial(_delta_kernel, hb=hb, nc=nc, chunk=chunk, zero_init=zero_init, emit_state=emit_state),
        grid=(b, N_HEAD // hb),
        in_specs=in_specs,
        out_specs=out_specs,
        out_shape=out_shape,
        scratch_shapes=[pltpu.VMEM((hb, HEAD, 2 * HEAD), F32),
                        pltpu.VMEM((hb, nc, chunk, HEAD), F32),
                        pltpu.VMEM((hb, nc, chunk, 2 * HEAD), F32),
                        pltpu.VMEM((hb, nc, chunk, 2 * HEAD), BF16),
                        pltpu.VMEM((hb, nc, chunk, 2 * chunk), F32),
                        pltpu.VMEM((hb, nc, chunk, HEAD), BF16),
                        pltpu.VMEM((hb, nc, chunk, HEAD), BF16)],
        compiler_params=_cparams("arbitrary", "arbitrary"),
        name="delta",
    )(*args)
    return (res[0], res[1]) if emit_state else (res[0], None)


def _out_kernel(a_ref, w_ref, x_ref, mod_ref, ng_ref, wr_ref, x1_ref, h2_ref, lg_ref):
    y = _dot(a_ref[0], w_ref[...])
    x1 = x_ref[0] + mod_ref[0, 2:3, :] * y
    x1_ref[0] = x1
    h2 = _rms_mod(x1, ng_ref[...], mod_ref[0, 4:5, :], mod_ref[0, 3:4, :])
    h2_ref[0] = h2.astype(BF16)
    lg_ref[0] = _dot3(wr_ref[...], h2, dot=_dot_nt)


def _out_proj(a, w_bf, x, mod, norm_g, wr_t):
    b, t, d = x.shape
    tm = ROW_TILE
    bc = mod.shape[0]
    mod_map = (lambda i, j: (i, 0, 0)) if bc > 1 else (lambda i, j: (0, 0, 0))
    tile = pl.BlockSpec((1, tm, d), lambda i, j: (i, j, 0))
    return pl.pallas_call(
        _out_kernel,
        grid=(b, t // tm),
        in_specs=[tile,
                  pl.BlockSpec((d, d), lambda i, j: (0, 0)),
                  tile,
                  pl.BlockSpec((1, 8, d), mod_map),
                  pl.BlockSpec((1, d), lambda i, j: (0, 0)),
                  pl.BlockSpec((N_EXP, d), lambda i, j: (0, 0))],
        out_specs=[tile, tile, pl.BlockSpec((1, N_EXP, tm), lambda i, j: (i, 0, j))],
        out_shape=[jax.ShapeDtypeStruct((b, t, d), F32),
                   jax.ShapeDtypeStruct((b, t, d), BF16),
                   jax.ShapeDtypeStruct((b, N_EXP, t), F32)],
        compiler_params=_cparams("arbitrary", "arbitrary"),
        name="out_proj",
    )(a, w_bf, x, mod, norm_g.reshape(1, d), wr_t)


def _route_kernel(lg_ref, slot_ref, aff_ref, *, cap):
    l = lg_ref[0]
    e, t = l.shape
    mx = jnp.max(l, axis=0, keepdims=True)
    ex = jnp.exp(l - mx)
    aff = ex / jnp.sum(ex, axis=0, keepdims=True)
    aff_ref[0] = aff
    blk = min(t, 256)
    s_loc = lax.broadcasted_iota(I32, (blk, t), 0)
    t_idx = lax.broadcasted_iota(I32, (blk, t), 1)
    diag = lax.broadcasted_iota(I32, (blk, blk), 0) == lax.broadcasted_iota(I32, (blk, blk), 1)
    capf = float(cap)

    def body(ei, carry):
        row = aff_ref[0, pl.ds(ei, 1), :]
        cnt = jnp.zeros((1, t), F32)
        for j in range(t // blk):
            cj = jnp.sum(jnp.where(diag, row[:, j * blk:(j + 1) * blk], 0.0), axis=1, keepdims=True)
            before = (s_loc + j * blk) < t_idx
            ahead = jnp.logical_or(cj > row, jnp.logical_and(cj == row, before))
            cnt = cnt + jnp.sum(jnp.where(ahead, 1.0, 0.0), axis=0, keepdims=True)
        slot_ref[0, pl.ds(ei, 1), :] = jnp.where(cnt < capf, cnt, -1.0).astype(I32)
        return carry

    lax.fori_loop(0, e, body, 0)


def _route(logits_t, cap):
    b, e, t = logits_t.shape
    spec = pl.BlockSpec((1, e, t), lambda i: (i, 0, 0))
    return pl.pallas_call(
        functools.partial(_route_kernel, cap=cap),
        grid=(b,),
        in_specs=[spec],
        out_specs=[spec, spec],
        out_shape=[jax.ShapeDtypeStruct((b, e, t), I32), jax.ShapeDtypeStruct((b, e, t), F32)],
        compiler_params=_cparams("arbitrary"),
        name="route",
    )(logits_t)


def _gather_kernel(h_ref, slot_ref, aff_ref, xs_ref, gs_ref, *, cap):
    t = h_ref.shape[1]
    rows = lax.broadcasted_iota(I32, (cap, t), 0)

    def body(e, carry):
        hit = rows == slot_ref[0, e]
        p = jnp.where(hit, 1.0, 0.0).astype(BF16)
        xs_ref[e, 0] = _dot(p, h_ref[0]).astype(BF16)
        gs_ref[e, 0] = jnp.sum(jnp.where(hit, aff_ref[0, e], 0.0), axis=1, keepdims=True)
        return carry

    lax.fori_loop(0, N_EXP, body, 0)


def _gather(h2, slotm, aff_t, cap):
    b, t, d = h2.shape
    e = N_EXP
    row_spec = pl.BlockSpec((1, e, 1, t), lambda i: (i, 0, 0, 0))
    return pl.pallas_call(
        functools.partial(_gather_kernel, cap=cap),
        grid=(b,),
        in_specs=[pl.BlockSpec((1, t, d), lambda i: (i, 0, 0)), row_spec, row_spec],
        out_specs=[pl.BlockSpec((e, 1, cap, d), lambda i: (0, i, 0, 0)),
                   pl.BlockSpec((e, 1, cap, 1), lambda i: (0, i, 0, 0))],
        out_shape=[jax.ShapeDtypeStruct((e, b, cap, d), BF16),
                   jax.ShapeDtypeStruct((e, b, cap, 1), F32)],
        compiler_params=_cparams("arbitrary"),
        name="gather",
    )(h2, slotm.reshape(b, e, 1, t), aff_t.reshape(b, e, 1, t))


def _ffn_kernel(xc_ref, gc_ref, xs_ref, gs_ref, wg_ref, wu_ref, wd_ref, yc_ref, ys_ref):
    wg = wg_ref[0, 0].astype(BF16)
    wu = wu_ref[0, 0].astype(BF16)
    wd = wd_ref[0, 0].astype(BF16)
    for x_ref, g_ref, y_ref in ((xc_ref, gc_ref, yc_ref), (xs_ref, gs_ref, ys_ref)):
        rows = x_ref.shape[1]
        rc = min(256, rows)
        for r0 in range(0, rows, rc):
            x = x_ref[0, r0:r0 + rc]
            hdn = (_silu(_dot(x, wg)) * _dot(x, wu)).astype(BF16)
            y_ref[0, r0:r0 + rc] = (_dot(hdn, wd) * g_ref[0, r0:r0 + rc]).astype(BF16)


def _ffn(xc, gc, xs, gs, w_gate, w_up, w_down, layer):
    e, rc, d = xc.shape
    rs = xs.shape[1]
    f = w_gate.shape[-1]
    wmap = lambda i: (layer, i, 0, 0)
    return pl.pallas_call(
        _ffn_kernel,
        grid=(e,),
        in_specs=[pl.BlockSpec((1, rc, d), lambda i: (i, 0, 0)),
                  pl.BlockSpec((1, rc, 1), lambda i: (i, 0, 0)),
                  pl.BlockSpec((1, rs, d), lambda i: (i, 0, 0)),
                  pl.BlockSpec((1, rs, 1), lambda i: (i, 0, 0)),
                  pl.BlockSpec((1, 1, d, f), wmap),
                  pl.BlockSpec((1, 1, d, f), wmap),
                  pl.BlockSpec((1, 1, f, d), wmap)],
        out_specs=[pl.BlockSpec((1, rc, d), lambda i: (i, 0, 0)),
                   pl.BlockSpec((1, rs, d), lambda i: (i, 0, 0))],
        out_shape=[jax.ShapeDtypeStruct((e, rc, d), BF16),
                   jax.ShapeDtypeStruct((e, rs, d), BF16)],
        compiler_params=_cparams("arbitrary"),
        name="expert_ffn",
    )(xc, gc, xs, gs, w_gate, w_up, w_down)


def _scatter_kernel(y_ref, slot_ref, x1_ref, mod_ref, fg_ref, o_ref, *, cap, final):
    tq = x1_ref.shape[1]
    e = N_EXP
    slot = slot_ref[0]
    if e * cap <= 1024:
        lanes = lax.broadcasted_iota(I32, (tq, e * cap), 1)
        hit = None
        for ee in range(e):
            gs = jnp.where(slot[:, ee:ee + 1] >= 0, slot[:, ee:ee + 1] + ee * cap, -1)
            m = lanes == gs
            hit = m if hit is None else jnp.logical_or(hit, m)
        pt = jnp.where(hit, 1.0, 0.0).astype(BF16)
        acc = _dot(pt, y_ref[:, 0].reshape(e * cap, -1))
    else:
        lanes = lax.broadcasted_iota(I32, (tq, cap), 1)
        acc = None
        for ee in range(e):
            pt = jnp.where(lanes == slot[:, ee:ee + 1], 1.0, 0.0).astype(BF16)
            part = _dot(pt, y_ref[ee, 0])
            acc = part if acc is None else acc + part
    x2 = x1_ref[0] + mod_ref[0, 5:6, :] * acc
    if final:
        x2 = x2 * lax.rsqrt(jnp.mean(x2 * x2, -1, keepdims=True) + EPS) * fg_ref[...]
    o_ref[0] = x2


def _scatter(y, slot_col, x1, mod, final_g, cap, final):
    b, t, d = x1.shape
    e = N_EXP
    tq = min(t, 512)
    bc = mod.shape[0]
    mod_map = (lambda i, j: (i, 0, 0)) if bc > 1 else (lambda i, j: (0, 0, 0))
    return pl.pallas_call(
        functools.partial(_scatter_kernel, cap=cap, final=final),
        grid=(b, t // tq),
        in_specs=[pl.BlockSpec((e, 1, cap, d), lambda i, j: (0, i, 0, 0)),
                  pl.BlockSpec((1, tq, e), lambda i, j: (i, j, 0)),
                  pl.BlockSpec((1, tq, d), lambda i, j: (i, j, 0)),
                  pl.BlockSpec((1, 8, d), mod_map),
                  pl.BlockSpec((1, d), lambda i, j: (0, 0))],
        out_specs=pl.BlockSpec((1, tq, d), lambda i, j: (i, j, 0)),
        out_shape=jax.ShapeDtypeStruct((b, t, d), F32),
        compiler_params=_cparams("arbitrary", "arbitrary"),
        name="scatter",
    )(y, slot_col, x1, mod, final_g.reshape(1, d))


def kernel(x_prompt, x_sample, state_mlstm_C, state_mlstm_n, state_mlstm_m, state_delta_S, c, c_ctx, norm1_g, norm2_g, w_ada, b_ada, w_in, w_out, w_router, w_gate, w_up, w_down, final_g, mlstm_ig_b, mlstm_fg_b, mlstm_norm_g, delta_conv_w, delta_A_log, delta_dt_bias, delta_norm_g):
    depth = w_in.shape[0]
    d = x_prompt.shape[-1]
    nb_s = x_sample.shape[0]
    cond8 = jnp.zeros((8, d), F32).at[0].set(c_ctx).at[1:1 + nb_s].set(c)
    mods = _modulation(cond8, w_ada, b_ada).reshape(depth, 8, 6, d)
    mods = jnp.pad(mods, ((0, 0), (0, 0), (0, 2), (0, 0)))
    xs = [x_prompt, x_sample]
    conv_rows = [x_prompt.shape[1], GRID_W]
    new_states = {}
    for i in range(depth):
        mod = [mods[i, 0:1], mods[i, 1:1 + nb_s]]
        w_qkvo = w_in[i, :, :4 * d].astype(BF16)
        w_g = w_in[i, :, 4 * d:]
        w_o = w_out[i].astype(BF16)
        wr_t = w_router[i].T
        j = i // 2
        routed = []
        for p in range(2):
            x = xs[p]
            b, t, _ = x.shape
            cap = CAP_FACTOR * t // N_EXP
            if i % 2 == 0:
                q, k, v, og, gates = _in_proj(x, mod[p], norm1_g[i], w_qkvo, w_g, None, "mlstm", None)
                state = None if p == 0 else (state_mlstm_C[:, j], state_mlstm_n[:, j], state_mlstm_m[:, j])
                a, st = _mlstm(q, k, v, og, gates, mlstm_ig_b[j], mlstm_fg_b[j], mlstm_norm_g[j], state, p == 0)
                if p == 0:
                    new_states.setdefault("m", []).append(st)
            else:
                q, k, v, og, gates = _in_proj(x, mod[p], norm1_g[i], w_qkvo, w_g, delta_conv_w[j], "delta", conv_rows[p])
                state = None if p == 0 else state_delta_S[:, j]
                a, st = _delta(q, k, v, og, gates, delta_A_log[j], delta_dt_bias[j], delta_norm_g[j], state, p == 0)
                if p == 0:
                    new_states.setdefault("d", []).append(st)
            x1, h2, logits_t = _out_proj(a, w_o, x, mod[p], norm2_g[i], wr_t)
            slotm, aff_t = _route(logits_t, cap)
            xg, gsel = _gather(h2, slotm, aff_t, cap)
            routed.append((x1, slotm, xg, gsel, cap, b))
        (x1c, slc, xgc, gsc, capc, bc_), (x1s, sls, xgs, gss, caps, bs_) = routed
        yc, ys = _ffn(xgc.reshape(N_EXP, bc_ * capc, d), gsc.reshape(N_EXP, bc_ * capc, 1),
                      xgs.reshape(N_EXP, bs_ * caps, d), gss.reshape(N_EXP, bs_ * caps, 1),
                      w_gate, w_up, w_down, i)
        final = i == depth - 1
        xs = [_scatter(yc.reshape(N_EXP, bc_, capc, d), slc.transpose(0, 2, 1), x1c, mod[0], final_g, capc, final),
              _scatter(ys.reshape(N_EXP, bs_, caps, d), sls.transpose(0, 2, 1), x1s, mod[1], final_g, caps, final)]
    ms = new_states["m"]
    new_c = jnp.stack([s[0] for s in ms], 1)
    new_n = jnp.stack([s[1] for s in ms], 1)
    new_m = jnp.stack([s[2] for s in ms], 1)
    new_s = jnp.stack(new_states["d"], 1)
    return (xs[0], xs[1], new_c, new_n, new_m, new_s)
```

```python
import functools
import math

import jax
import jax.numpy as jnp
from jax import lax
from jax.experimental import pallas as pl
from jax.experimental.pallas import tpu as pltpu

F32 = jnp.float32
BF16 = jnp.bfloat16
I32 = jnp.int32

N_HEAD = 8
HEAD = 128
N_EXP = 16
CAP_FACTOR = 2
GRID_W = 64
CONV_W = 5
DELTA_CHUNK = 64
MLSTM_CHUNK = 128
EPS = 1e-6
ROW_TILE = 256
VMEM_LIMIT = 56 * 1024 * 1024


def _cparams(*sem):
    return pltpu.CompilerParams(dimension_semantics=sem, vmem_limit_bytes=VMEM_LIMIT)


def _dot(a, b):
    return jnp.dot(a, b, preferred_element_type=F32)


def _dot_nt(a, b):
    return lax.dot_general(a, b, (((1,), (1,)), ((), ())), preferred_element_type=F32)


def _dot_tn(a, b):
    return lax.dot_general(a, b, (((0,), (0,)), ((), ())), preferred_element_type=F32)


def _split(x):
    hi = x.astype(BF16)
    lo = (x - hi.astype(F32)).astype(BF16)
    return hi, lo


def _dot3(a, b, dot=_dot):
    ah, al = _split(a)
    bh, bl = _split(b)
    return dot(ah, bh) + (dot(ah, bl) + dot(al, bh))


def _sigmoid(x):
    return 1.0 / (1.0 + jnp.exp(-x))


def _silu(x):
    return x * _sigmoid(x)


def _softplus(x):
    return jnp.maximum(x, 0.0) + jnp.log1p(jnp.exp(-jnp.abs(x)))


def _log_sigmoid(x):
    return -_softplus(-x)


def _rms_mod(x, g, scale, shift):
    y = x * lax.rsqrt(jnp.mean(x * x, -1, keepdims=True) + EPS)
    return (y * g) * (1.0 + scale) + shift


def _mod_kernel(cond_ref, w_ref, b_ref, o_ref):
    c = cond_ref[...]
    o_ref[0] = _dot3(_silu(c), w_ref[0]) + b_ref[0]


def _modulation(cond8, w_ada, b_ada):
    depth, d, n6 = w_ada.shape
    tn = 1536
    return pl.pallas_call(
        _mod_kernel,
        grid=(depth, n6 // tn),
        in_specs=[pl.BlockSpec((8, d), lambda l, j: (0, 0)),
                  pl.BlockSpec((1, d, tn), lambda l, j: (l, 0, j)),
                  pl.BlockSpec((1, 1, tn), lambda l, j: (l, 0, j))],
        out_specs=pl.BlockSpec((1, 8, tn), lambda l, j: (l, 0, j)),
        out_shape=jax.ShapeDtypeStruct((depth, 8, n6), F32),
        compiler_params=_cparams("arbitrary", "arbitrary"),
        name="modulation",
    )(cond8, w_ada, b_ada.reshape(depth, 1, n6))


def _in_kernel(x_ref, mod_ref, ng_ref, w_ref, wg_ref, *rest, kind, conv_row):
    if kind == "delta":
        cw_ref, q_ref, k_ref, v_ref, og_ref, gates_ref = rest
    else:
        q_ref, k_ref, v_ref, og_ref, gates_ref = rest
    x = x_ref[0]
    tm, d = x.shape
    h = _rms_mod(x, ng_ref[...], mod_ref[0, 1:2, :], mod_ref[0, 0:1, :])
    hb = h.astype(BF16)
    gates_ref[0] = _dot3(h, wg_ref[...])
    if kind == "delta":
        pos = lax.broadcasted_iota(I32, (tm, 1), 0) % conv_row
    for j in range(4):
        acc = _dot(hb, w_ref[:, j * d:(j + 1) * d])
        if kind == "mlstm":
            if j == 1:
                acc = acc * (HEAD ** -0.5)
            if j == 3:
                og_ref[0] = _sigmoid(acc).astype(BF16)
                continue
        else:
            if j == 3:
                og_ref[0] = _silu(acc).astype(BF16)
                continue
            cw = cw_ref[:, j * d:(j + 1) * d]
            out = acc * cw[2:3, :]
            for tap, sft in ((0, -2), (1, -1), (3, 1), (4, 2)):
                shifted = pltpu.roll(acc, (-sft) % tm, 0)
                valid = jnp.logical_and(pos + sft >= 0, pos + sft < conv_row)
                out = out + jnp.where(valid, shifted, 0.0) * cw[tap:tap + 1, :]
            acc = _silu(out)
        dst = (q_ref, k_ref, v_ref)[j]
        for hh in range(N_HEAD):
            blk = acc[:, hh * HEAD:(hh + 1) * HEAD]
            if kind == "delta" and j < 2:
                blk = blk * lax.rsqrt(jnp.sum(blk * blk, -1, keepdims=True) + EPS)
                if j == 0:
                    blk = blk * (HEAD ** -0.5)
            dst[0, hh] = blk.astype(BF16)


def _in_proj(x, mod, norm_g, w_bf, w_gate, conv_w, kind, conv_row):
    b, t, d = x.shape
    tm = ROW_TILE
    nt = t // tm
    bc = mod.shape[0]
    mod_map = (lambda i, j: (i, 0, 0)) if bc > 1 else (lambda i, j: (0, 0, 0))
    in_specs = [pl.BlockSpec((1, tm, d), lambda i, j: (i, j, 0)),
                pl.BlockSpec((1, 8, d), mod_map),
                pl.BlockSpec((1, d), lambda i, j: (0, 0)),
                pl.BlockSpec((d, 4 * d), lambda i, j: (0, 0)),
                pl.BlockSpec((d, 32), lambda i, j: (0, 0))]
    args = [x, mod, norm_g.reshape(1, d), w_bf, w_gate]
    if kind == "delta":
        in_specs.append(pl.BlockSpec((CONV_W, 3 * d), lambda i, j: (0, 0)))
        args.append(conv_w)
    hd_spec = pl.BlockSpec((1, N_HEAD, tm, HEAD), lambda i, j: (i, 0, j, 0))
    hd_shape = jax.ShapeDtypeStruct((b, N_HEAD, t, HEAD), BF16)
    return pl.pallas_call(
        functools.partial(_in_kernel, kind=kind, conv_row=conv_row),
        grid=(b, nt),
        in_specs=in_specs,
        out_specs=[hd_spec, hd_spec, hd_spec,
                   pl.BlockSpec((1, tm, d), lambda i, j: (i, j, 0)),
                   pl.BlockSpec((1, tm, 32), lambda i, j: (i, j, 0))],
        out_shape=[hd_shape, hd_shape, hd_shape,
                   jax.ShapeDtypeStruct((b, t, d), BF16),
                   jax.ShapeDtypeStruct((b, t, 32), F32)],
        compiler_params=_cparams("arbitrary", "arbitrary"),
        name="in_proj_" + kind,
    )(*args)


def _gate_layouts(gates, nc, chunk):
    b, t, _ = gates.shape
    g = gates.reshape(b, nc, chunk, 4, N_HEAD)
    return g.transpose(0, 4, 1, 2, 3), g.transpose(0, 4, 1, 3, 2)


def _mlstm_kernel(*refs, hb, nc, chunk, zero_init, emit_state):
    it = iter(refs)
    q_ref, k_ref, v_ref, og_ref, gcol_ref, grow_ref, igb_ref, fgb_ref, ng_ref = (next(it) for _ in range(9))
    if not zero_init:
        c0_ref, n0_ref, m0_ref = (next(it) for _ in range(3))
    a_ref = next(it)
    if emit_state:
        cn_ref, nn_ref, mn_ref = (next(it) for _ in range(3))
    c_s, n_s, m_s, hf_s, hr_s = (next(it) for _ in range(5))
    L = chunk
    hg = pl.program_id(1)
    if zero_init:
        c_s[...] = jnp.zeros_like(c_s)
        n_s[...] = jnp.zeros_like(n_s)
        m_s[...] = jnp.zeros_like(m_s)
    else:
        c_s[...] = c0_ref[0]
        n_s[...] = n0_ref[0]
        m_s[...] = m0_ref[0]
    row = lax.broadcasted_iota(I32, (L, L), 0)
    col = lax.broadcasted_iota(I32, (L, L), 1)
    tris = (row >= col, col >= row)

    def body(c, carry):
        chains = [(d, hh) for d in range(2) for hh in range(hb)]
        st = []
        for d, hh in chains:
            cc = c if d == 0 else nc - 1 - c
            tri = tris[d]
            head = hg * hb + hh
            qc = q_ref[0, hh, cc]
            kc = k_ref[0, hh, cc]
            vc = v_ref[0, hh, cc]
            gcol = gcol_ref[0, hh, cc]
            grw = grow_ref[0, hh, cc]
            igb = igb_ref[d, head]
            fgb = fgb_ref[d, head]
            i_col = gcol[:, 2 * d:2 * d + 1] + igb
            f_col = _log_sigmoid(gcol[:, 2 * d + 1:2 * d + 2] + fgb)
            i_row = grw[2 * d:2 * d + 1, :] + igb
            f_row = _log_sigmoid(grw[2 * d + 1:2 * d + 2, :] + fgb)
            b_col = jnp.sum(jnp.where(tri, f_row, 0.0), axis=1, keepdims=True)
            b_row = jnp.sum(jnp.where(tris[1 - d], f_col, 0.0), axis=0, keepdims=True)
            bl = jnp.sum(f_row, axis=1, keepdims=True)
            m_prev = m_s[d, hh]
            c_prev = c_s[d, hh]
            n_prev = n_s[d, hh]
            logd = jnp.where(tri, b_col - b_row + i_row, -jnp.inf)
            inter = b_col + m_prev
            mt = jnp.maximum(inter, jnp.max(logd, axis=1, keepdims=True))
            logw = bl - b_col + i_col
            m_new = jnp.maximum(bl + m_prev, jnp.max(logw, axis=0, keepdims=True))
            kw = kc.astype(F32) * jnp.exp(logw - m_new)
            st.append(dict(cc=cc, qc=qc, kc=kc, vc=vc, logd=logd, inter=inter, mt=mt, m_new=m_new, kw=kw,
                           dec=jnp.exp(bl + m_prev - m_new), c_prev=c_prev, n_prev=n_prev))
        for z in st:
            z["qk"] = _dot_nt(z["qc"], z["kc"])
            z["qc_c"] = _dot(z["qc"], z["c_prev"].astype(BF16))
        for z in st:
            z["s"] = z["qk"] * jnp.exp(z["logd"] - z["mt"])
            z["sv"] = _dot(z["s"].astype(BF16), z["vc"])
        for z in st:
            z["kv"] = _dot_tn(z["kw"].astype(BF16), z["vc"])
        for (d, hh), z in zip(chains, st):
            sc = jnp.exp(z["inter"] - z["mt"])
            num = z["sv"] + sc * z["qc_c"]
            den = (jnp.sum(z["s"], axis=1, keepdims=True)
                   + sc * jnp.sum(z["qc"].astype(F32) * z["n_prev"], axis=1, keepdims=True))
            hout = num / jnp.maximum(jnp.abs(den), jnp.exp(-z["mt"]))
            if d == 0:
                hf_s[hh, z["cc"]] = hout
            else:
                hr_s[hh, z["cc"]] = hout
            c_s[d, hh] = z["dec"] * z["c_prev"] + z["kv"]
            n_s[d, hh] = z["dec"] * z["n_prev"] + jnp.sum(z["kw"], axis=0, keepdims=True)
            m_s[d, hh] = z["m_new"]
        return carry

    lax.fori_loop(0, nc, body, 0)

    def emit(c, carry):
        t0 = pl.multiple_of(c * L, L)
        for hh in range(hb):
            tot = hf_s[hh, c] + hr_s[hh, c]
            hn = tot * lax.rsqrt(jnp.mean(tot * tot, -1, keepdims=True) + EPS) * ng_ref[:, hh * HEAD:(hh + 1) * HEAD]
            gate = og_ref[0, pl.ds(t0, L), hh * HEAD:(hh + 1) * HEAD].astype(F32)
            a_ref[0, pl.ds(t0, L), hh * HEAD:(hh + 1) * HEAD] = (gate * hn).astype(BF16)
        return carry

    lax.fori_loop(0, nc, emit, 0)
    if emit_state:
        cn_ref[0] = c_s[...]
        nn_ref[0] = n_s[...]
        mn_ref[0] = m_s[...]


def _mlstm(q, k, v, og, gates, ig_b, fg_b, norm_g, state, emit_state):
    b, _, t, _ = q.shape
    d = N_HEAD * HEAD
    chunk = min(t, MLSTM_CHUNK)
    nc = t // chunk
    hb = 4
    gcol, grow = _gate_layouts(gates, nc, chunk)
    qkv_spec = pl.BlockSpec((1, hb, nc, chunk, HEAD), lambda i, j: (i, j, 0, 0, 0))
    r5 = lambda a: a.reshape(b, N_HEAD, nc, chunk, HEAD)
    smem = pl.BlockSpec(memory_space=pltpu.SMEM)
    in_specs = [qkv_spec, qkv_spec, qkv_spec,
                pl.BlockSpec((1, t, hb * HEAD), lambda i, j: (i, 0, j)),
                pl.BlockSpec((1, hb, nc, chunk, 4), lambda i, j: (i, j, 0, 0, 0)),
                pl.BlockSpec((1, hb, nc, 4, chunk), lambda i, j: (i, j, 0, 0, 0)),
                smem, smem,
                pl.BlockSpec((1, hb * HEAD), lambda i, j: (0, j))]
    args = [r5(q), r5(k), r5(v), og, gcol, grow, ig_b, fg_b, norm_g.reshape(1, d)]
    zero_init = state is None
    c_spec = pl.BlockSpec((1, 2, hb, HEAD, HEAD), lambda i, j: (i, 0, j, 0, 0))
    n_spec = pl.BlockSpec((1, 2, hb, 1, HEAD), lambda i, j: (i, 0, j, 0, 0))
    m_spec = pl.BlockSpec((1, 2, hb, 1, 1), lambda i, j: (i, 0, j, 0, 0))
    if not zero_init:
        c0, n0, m0 = state
        in_specs += [c_spec, n_spec, m_spec]
        args += [c0, n0.reshape(b, 2, N_HEAD, 1, HEAD), m0.reshape(b, 2, N_HEAD, 1, 1)]
    out_specs = [pl.BlockSpec((1, t, hb * HEAD), lambda i, j: (i, 0, j))]
    out_shape = [jax.ShapeDtypeStruct((b, t, d), BF16)]
    if emit_state:
        out_specs += [c_spec, n_spec, m_spec]
        out_shape += [jax.ShapeDtypeStruct((b, 2, N_HEAD, HEAD, HEAD), F32),
                      jax.ShapeDtypeStruct((b, 2, N_HEAD, 1, HEAD), F32),
                      jax.ShapeDtypeStruct((b, 2, N_HEAD, 1, 1), F32)]
    res = pl.pallas_call(
        functools.partial(_mlstm_kernel, hb=hb, nc=nc, chunk=chunk, zero_init=zero_init, emit_state=emit_state),
        grid=(b, N_HEAD // hb),
        in_specs=in_specs,
        out_specs=out_specs,
        out_shape=out_shape,
        scratch_shapes=[pltpu.VMEM((2, hb, HEAD, HEAD), F32),
                        pltpu.VMEM((2, hb, 1, HEAD), F32),
                        pltpu.VMEM((2, hb, 1, 1), F32),
                        pltpu.VMEM((hb, nc, chunk, HEAD), F32),
                        pltpu.VMEM((hb, nc, chunk, HEAD), F32)],
        compiler_params=_cparams("arbitrary", "arbitrary"),
        name="mlstm",
    )(*args)
    if emit_state:
        a, cn, nn, mn = res
        return a, (cn, nn.reshape(b, 2, N_HEAD, HEAD), mn.reshape(b, 2, N_HEAD))
    return res[0], None


def _delta_kernel(*refs, hb, nc, chunk, zero_init, emit_state):
    it = iter(refs)
    q_ref, k_ref, v_ref, zs_ref, grow_ref, alog_ref, dtb_ref, ng_ref = (next(it) for _ in range(8))
    if not zero_init:
        s0_ref = next(it)
    a_ref = next(it)
    if emit_state:
        sn_ref = next(it)
    s_s, o_s, u_s, wk_s, qk_s, kdf_s, kdb_s = (next(it) for _ in range(7))
    L = chunk
    hg = pl.program_id(1)
    for hh in range(hb):
        if zero_init:
            s_s[hh] = jnp.zeros((HEAD, 2 * HEAD), F32)
        else:
            s_s[hh] = jnp.concatenate([s0_ref[0, 0, hh], s0_ref[0, 1, hh]], axis=1)
    o_s[...] = jnp.zeros_like(o_s)
    t_i = lax.broadcasted_iota(I32, (L, 2 * L), 0)
    lane = lax.broadcasted_iota(I32, (L, 2 * L), 1)
    lo = lane < L
    s_i = jnp.where(lo, lane, lane - L)
    lo_row = lax.broadcasted_iota(I32, (1, 2 * L), 1) < L
    hi = jnp.logical_not(lo)

    def bsel(a, b):
        return jnp.logical_or(jnp.logical_and(lo, a), jnp.logical_and(hi, b))

    incl = bsel(s_i <= t_i, s_i >= t_i)
    strict = bsel(s_i < t_i, s_i > t_i)
    eye2 = s_i == t_i
    n_lvl = int(math.log2(L))
    lvls = []
    for kbit in range(n_lvl):
        same = (t_i >> (kbit + 1)) == (s_i >> (kbit + 1))
        tb = ((t_i >> kbit) & 1) == 1
        sb = ((s_i >> kbit) & 1) == 1
        fwd_m = jnp.logical_and(tb, jnp.logical_not(sb))
        bwd_m = jnp.logical_and(sb, jnp.logical_not(tb))
        lvls.append(jnp.logical_and(same, bsel(fwd_m, bwd_m)))
    zero_l = jnp.zeros((L, HEAD), F32)

    def half_sums(z):
        return (jnp.sum(jnp.where(lo, z, 0.0), axis=1, keepdims=True),
                jnp.sum(jnp.where(lo, 0.0, z), axis=1, keepdims=True))

    def gate_cum(hh, cc):
        head = hg * hb + hh
        rows = grow_ref[0, hh, cc]
        a2 = -jnp.exp(jnp.where(lo_row, alog_ref[0, head], alog_ref[1, head]))
        dtb2 = jnp.where(lo_row, dtb_ref[0, head], dtb_ref[1, head])
        g_row = a2 * _softplus(rows[0:1, :] + dtb2)
        gc_f, gc_b = half_sums(jnp.where(incl, g_row, 0.0))
        gt_f = jnp.sum(jnp.where(lo_row, g_row, 0.0), axis=1, keepdims=True)
        gt_b = jnp.sum(jnp.where(lo_row, 0.0, g_row), axis=1, keepdims=True)
        return rows, gc_f, gc_b, gt_f, gt_b

    def pair_dot(p, q):
        bd = jnp.concatenate([jnp.where(lo, q, 0.0), jnp.where(lo, 0.0, q)], axis=0).astype(BF16)
        return _dot(p.astype(BF16), bd)

    def local_body(c, carry):
        chains = [(hh, 2 * c + u) for u in range(2) for hh in range(hb)]
        st = []
        for hh, cc in chains:
            qc = q_ref[0, hh, cc]
            kc = k_ref[0, hh, cc]
            rows, gc_f, gc_b, gt_f, gt_b = gate_cum(hh, cc)
            gc2 = jnp.where(lo, gc_f, gc_b)
            gr2 = jnp.sum(jnp.where(eye2, gc2, 0.0), axis=0, keepdims=True)
            beta_f, beta_b = half_sums(jnp.where(eye2, _sigmoid(rows[1:2, :]), 0.0))
            decay = jnp.exp(jnp.where(incl, gc2 - gr2, -jnp.inf))
            big = _dot_nt(jnp.concatenate([qc, kc], axis=0), jnp.concatenate([kc, kc], axis=0))
            st.append(dict(hh=hh, cc=cc, kc=kc, gc_f=gc_f, gc_b=gc_b, gt_f=gt_f, gt_b=gt_b,
                           beta_f=beta_f, beta_b=beta_b, decay=decay, big=big))
        for z in st:
            qk_s[z["hh"], z["cc"]] = z["big"][:L] * z["decay"]
            z["amat"] = jnp.where(strict, z["big"][L:] * z["decay"] * jnp.where(lo, z["beta_f"], z["beta_b"]), 0.0)
            z["rinv"] = -jnp.where(lvls[0], z["amat"], 0.0)
        for kbit in range(1, n_lvl):
            for z in st:
                lk = jnp.where(lvls[kbit], z["amat"], 0.0)
                z["x"] = lk + pair_dot(z["rinv"], lk)
            for z in st:
                z["rinv"] = z["rinv"] - (z["x"] + pair_dot(z["x"], z["rinv"]))
        zero2 = jnp.zeros((L, 2 * HEAD), F32)
        for z in st:
            kcf = z["kc"].astype(F32)
            vcf = v_ref[0, z["hh"], z["cc"]].astype(F32)
            bf_, bb_ = z["beta_f"], z["beta_b"]
            z["rhs_f"] = jnp.concatenate([vcf * bf_, kcf * (bf_ * jnp.exp(z["gc_f"]))], axis=1)
            z["rhs_b"] = jnp.concatenate([vcf * bb_, kcf * (bb_ * jnp.exp(z["gc_b"]))], axis=1)
            wbd = jnp.concatenate([jnp.concatenate([z["rhs_f"], zero2], axis=1),
                                   jnp.concatenate([zero2, z["rhs_b"]], axis=1)], axis=0).astype(BF16)
            z["uw"] = _dot(z["rinv"].astype(BF16), wbd)
            kdf_s[z["hh"], z["cc"]] = (kcf * jnp.exp(z["gt_f"] - z["gc_f"])).astype(BF16)
            kdb_s[z["hh"], z["cc"]] = (kcf * jnp.exp(z["gt_b"] - z["gc_b"])).astype(BF16)
        for z in st:
            uw, rhs_f, rhs_b = z["uw"], z["rhs_f"], z["rhs_b"]
            u_s[z["hh"], z["cc"]] = jnp.concatenate([rhs_f[:, :HEAD] + uw[:, :HEAD],
                                                    rhs_b[:, :HEAD] + uw[:, 2 * HEAD:3 * HEAD]], axis=1)
            wk_s[z["hh"], z["cc"]] = jnp.concatenate([rhs_f[:, HEAD:] + uw[:, HEAD:2 * HEAD],
                                                     rhs_b[:, HEAD:] + uw[:, 3 * HEAD:]], axis=1).astype(BF16)
        return carry

    lax.fori_loop(0, nc // 2, local_body, 0)

    lane2 = lax.broadcasted_iota(I32, (1, 2 * HEAD), 1)

    def scan_body(c, carry):
        cr = nc - 1 - c
        st = []
        for hh in range(hb):
            _, gc_f, _, gt_f, _ = gate_cum(hh, c)
            _, _, gc_b, _, gt_b = gate_cum(hh, cr)
            s2 = s_s[hh]
            sb = s2.astype(BF16)
            ws_f = _dot(jnp.concatenate([wk_s[hh, c][:, :HEAD], q_ref[0, hh, c]], axis=0), sb[:, :HEAD])
            ws_b = _dot(jnp.concatenate([wk_s[hh, cr][:, HEAD:], q_ref[0, hh, cr]], axis=0), sb[:, HEAD:])
            st.append(dict(hh=hh, gc_f=gc_f, gc_b=gc_b, gt_f=gt_f, gt_b=gt_b, s2=s2, ws_f=ws_f, ws_b=ws_b))
        for z in st:
            hh = z["hh"]
            w_f = u_s[hh, c][:, :HEAD] - z["ws_f"][:L]
            w_b = u_s[hh, cr][:, HEAD:] - z["ws_b"][:L]
            z["wbd"] = jnp.concatenate([jnp.concatenate([w_f, zero_l], axis=1),
                                        jnp.concatenate([zero_l, w_b], axis=1)], axis=0).astype(BF16)
            qkp = jnp.where(lo, qk_s[hh, c], qk_s[hh, cr]).astype(BF16)
            z["ow"] = _dot(qkp, z["wbd"])
        for z in st:
            hh = z["hh"]
            kd2 = jnp.concatenate([kdf_s[hh, c], kdb_s[hh, cr]], axis=0)
            egl = jnp.exp(jnp.where(lane2 < HEAD, z["gt_f"], z["gt_b"]))
            s_s[hh] = egl * z["s2"] + _dot_tn(kd2, z["wbd"])
        for z in st:
            hh = z["hh"]
            o_s[hh, c] += jnp.exp(z["gc_f"]) * z["ws_f"][L:] + z["ow"][:, :HEAD]
            o_s[hh, cr] += jnp.exp(z["gc_b"]) * z["ws_b"][L:] + z["ow"][:, HEAD:]
        return carry

    lax.fori_loop(0, nc, scan_body, 0)

    def emit(c, carry):
        t0 = pl.multiple_of(c * L, L)
        for hh in range(hb):
            tot = o_s[hh, c]
            hn = tot * lax.rsqrt(jnp.mean(tot * tot, -1, keepdims=True) + EPS) * ng_ref[...]
            gate = zs_ref[0, pl.ds(t0, L), hh * HEAD:(hh + 1) * HEAD].astype(F32)
            a_ref[0, pl.ds(t0, L), hh * HEAD:(hh + 1) * HEAD] = (hn * gate).astype(BF16)
        return carry

    lax.fori_loop(0, nc, emit, 0)
    if emit_state:
        for hh in range(hb):
            sn_ref[0, 0, hh] = s_s[hh][:, :HEAD]
            sn_ref[0, 1, hh] = s_s[hh][:, HEAD:]


def _delta(q, k, v, zs, gates, a_log, dt_bias, norm_g, state, emit_state):
    b, _, t, _ = q.shape
    d = N_HEAD * HEAD
    chunk = DELTA_CHUNK
    nc = t // chunk
    hb = 4
    assert 2 * chunk == HEAD and nc % 2 == 0
    grow = gates.reshape(b, nc, chunk, 2, 2, N_HEAD).transpose(0, 5, 1, 4, 3, 2).reshape(b, N_HEAD, nc, 2, 2 * chunk)
    qkv_spec = pl.BlockSpec((1, hb, nc, chunk, HEAD), lambda i, j: (i, j, 0, 0, 0))
    r5 = lambda a: a.reshape(b, N_HEAD, nc, chunk, HEAD)
    smem = pl.BlockSpec(memory_space=pltpu.SMEM)
    in_specs = [qkv_spec, qkv_spec, qkv_spec,
                pl.BlockSpec((1, t, hb * HEAD), lambda i, j: (i, 0, j)),
                pl.BlockSpec((1, hb, nc, 2, 2 * chunk), lambda i, j: (i, j, 0, 0, 0)),
                smem, smem,
                pl.BlockSpec((1, HEAD), lambda i, j: (0, 0))]
    args = [r5(q), r5(k), r5(v), zs, grow, a_log, dt_bias, norm_g.reshape(1, HEAD)]
    zero_init = state is None
    s_spec = pl.BlockSpec((1, 2, hb, HEAD, HEAD), lambda i, j: (i, 0, j, 0, 0))
    if not zero_init:
        in_specs.append(s_spec)
        args.append(state)
    out_specs = [pl.BlockSpec((1, t, hb * HEAD), lambda i, j: (i, 0, j))]
    out_shape = [jax.ShapeDtypeStruct((b, t, d), BF16)]
    if emit_state:
        out_specs.append(s_spec)
        out_shape.append(jax.ShapeDtypeStruct((b, 2, N_HEAD, HEAD, HEAD), F32))
    res = pl.pallas_call(
        functools.partial(_delta_kernel, hb=hb, nc=nc, chunk=chunk, zero_init=zero_init, emit_state=emit_state),
        grid=(b, N_HEAD // hb),
        in_specs=in_specs,
        out_specs=out_specs,
        out_shape=out_shape,
        scratch_shapes=[pltpu.VMEM((hb, HEAD, 2 * HEAD), F32),
                        pltpu.VMEM((hb, nc, chunk, HEAD), F32),
                        pltpu.VMEM((hb, nc, chunk, 2 * HEAD), F32),
                        pltpu.VMEM((hb, nc, chunk, 2 * HEAD), BF16),
                        pltpu.VMEM((hb, nc, chunk, 2 * chunk), F32),
                        pltpu.VMEM((hb, nc, chunk, HEAD), BF16),
                        pltpu.VMEM((hb, nc, chunk, HEAD), BF16)],
        compiler_params=_cparams("arbitrary", "arbitrary"),
        name="delta",
    )(*args)
    return (res[0], res[1]) if emit_state else (res[0], None)


def _out_kernel(a_ref, w_ref, x_ref, mod_ref, ng_ref, wr_ref, x1_ref, h2_ref, lg_ref):
    y = _dot(a_ref[0], w_ref[...])
    x1 = x_ref[0] + mod_ref[0, 2:3, :] * y
    x1_ref[0] = x1
    h2 = _rms_mod(x1, ng_ref[...], mod_ref[0, 4:5, :], mod_ref[0, 3:4, :])
    h2_ref[0] = h2.astype(BF16)
    lg_ref[0] = _dot3(wr_ref[...], h2, dot=_dot_nt)


def _out_proj(a, w_bf, x, mod, norm_g, wr_t):
    b, t, d = x.shape
    tm = ROW_TILE
    bc = mod.shape[0]
    mod_map = (lambda i, j: (i, 0, 0)) if bc > 1 else (lambda i, j: (0, 0, 0))
    tile = pl.BlockSpec((1, tm, d), lambda i, j: (i, j, 0))
    return pl.pallas_call(
        _out_kernel,
        grid=(b, t // tm),
        in_specs=[tile,
                  pl.BlockSpec((d, d), lambda i, j: (0, 0)),
                  tile,
                  pl.BlockSpec((1, 8, d), mod_map),
                  pl.BlockSpec((1, d), lambda i, j: (0, 0)),
                  pl.BlockSpec((N_EXP, d), lambda i, j: (0, 0))],
        out_specs=[tile, tile, pl.BlockSpec((1, N_EXP, tm), lambda i, j: (i, 0, j))],
        out_shape=[jax.ShapeDtypeStruct((b, t, d), F32),
                   jax.ShapeDtypeStruct((b, t, d), BF16),
                   jax.ShapeDtypeStruct((b, N_EXP, t), F32)],
        compiler_params=_cparams("arbitrary", "arbitrary"),
        name="out_proj",
    )(a, w_bf, x, mod, norm_g.reshape(1, d), wr_t)


def _route_kernel(lg_ref, slot_ref, aff_ref, *, cap):
    l = lg_ref[0]
    e, t = l.shape
    mx = jnp.max(l, axis=0, keepdims=True)
    ex = jnp.exp(l - mx)
    aff = ex / jnp.sum(ex, axis=0, keepdims=True)
    aff_ref[0] = aff
    blk = min(t, 256)
    nb = t // blk
    s_loc = lax.broadcasted_iota(I32, (blk, blk), 0)
    t_loc = lax.broadcasted_iota(I32, (blk, blk), 1)
    diag = s_loc == t_loc
    lower = s_loc < t_loc
    capf = float(cap)

    def count(mask):
        return jnp.sum(jnp.where(mask, 1.0, 0.0), axis=0, keepdims=True)

    def body(ei, carry):
        row = aff_ref[0, pl.ds(ei, 1), :]
        cnt = jnp.zeros((1, t), F32)
        for j in range(nb):
            rd = row[:, j * blk:(j + 1) * blk]
            cj = jnp.sum(jnp.where(diag, rd, 0.0), axis=1, keepdims=True)
            parts = []
            if j > 0:
                parts.append(count(cj > row[:, :j * blk]))
            parts.append(count(jnp.logical_or(cj > rd, jnp.logical_and(cj == rd, lower))))
            if j < nb - 1:
                parts.append(count(cj >= row[:, (j + 1) * blk:]))
            cnt = cnt + (parts[0] if len(parts) == 1 else jnp.concatenate(parts, axis=1))
        slot_ref[0, pl.ds(ei, 1), :] = jnp.where(cnt < capf, cnt, -1.0).astype(I32)
        return carry

    lax.fori_loop(0, e, body, 0)


def _route(logits_t, cap):
    b, e, t = logits_t.shape
    spec = pl.BlockSpec((1, e, t), lambda i: (i, 0, 0))
    return pl.pallas_call(
        functools.partial(_route_kernel, cap=cap),
        grid=(b,),
        in_specs=[spec],
        out_specs=[spec, spec],
        out_shape=[jax.ShapeDtypeStruct((b, e, t), I32), jax.ShapeDtypeStruct((b, e, t), F32)],
        compiler_params=_cparams("arbitrary"),
        name="route",
    )(logits_t)


def _gather_kernel(h_ref, slot_ref, aff_ref, xs_ref, gs_ref, *, cap):
    t = h_ref.shape[1]
    rows = lax.broadcasted_iota(I32, (cap, t), 0)

    def body(e, carry):
        hit = rows == slot_ref[0, e]
        p = jnp.where(hit, 1.0, 0.0).astype(BF16)
        xs_ref[e, 0] = _dot(p, h_ref[0]).astype(BF16)
        gs_ref[e, 0] = jnp.sum(jnp.where(hit, aff_ref[0, e], 0.0), axis=1, keepdims=True)
        return carry

    lax.fori_loop(0, N_EXP, body, 0)


def _gather(h2, slotm, aff_t, cap):
    b, t, d = h2.shape
    e = N_EXP
    row_spec = pl.BlockSpec((1, e, 1, t), lambda i: (i, 0, 0, 0))
    return pl.pallas_call(
        functools.partial(_gather_kernel, cap=cap),
        grid=(b,),
        in_specs=[pl.BlockSpec((1, t, d), lambda i: (i, 0, 0)), row_spec, row_spec],
        out_specs=[pl.BlockSpec((e, 1, cap, d), lambda i: (0, i, 0, 0)),
                   pl.BlockSpec((e, 1, cap, 1), lambda i: (0, i, 0, 0))],
        out_shape=[jax.ShapeDtypeStruct((e, b, cap, d), BF16),
                   jax.ShapeDtypeStruct((e, b, cap, 1), F32)],
        compiler_params=_cparams("arbitrary"),
        name="gather",
    )(h2, slotm.reshape(b, e, 1, t), aff_t.reshape(b, e, 1, t))


def _ffn_kernel(xc_ref, gc_ref, xs_ref, gs_ref, wg_ref, wu_ref, wd_ref, yc_ref, ys_ref):
    wg = wg_ref[0, 0].astype(BF16)
    wu = wu_ref[0, 0].astype(BF16)
    wd = wd_ref[0, 0].astype(BF16)
    for x_ref, g_ref, y_ref in ((xc_ref, gc_ref, yc_ref), (xs_ref, gs_ref, ys_ref)):
        rows = x_ref.shape[1]
        rc = min(256, rows)
        for r0 in range(0, rows, rc):
            x = x_ref[0, r0:r0 + rc]
            hdn = (_silu(_dot(x, wg)) * _dot(x, wu)).astype(BF16)
            y_ref[0, r0:r0 + rc] = (_dot(hdn, wd) * g_ref[0, r0:r0 + rc]).astype(BF16)


def _ffn(xc, gc, xs, gs, w_gate, w_up, w_down, layer):
    e, rc, d = xc.shape
    rs = xs.shape[1]
    f = w_gate.shape[-1]
    wmap = lambda i: (layer, i, 0, 0)
    return pl.pallas_call(
        _ffn_kernel,
        grid=(e,),
        in_specs=[pl.BlockSpec((1, rc, d), lambda i: (i, 0, 0)),
                  pl.BlockSpec((1, rc, 1), lambda i: (i, 0, 0)),
                  pl.BlockSpec((1, rs, d), lambda i: (i, 0, 0)),
                  pl.BlockSpec((1, rs, 1), lambda i: (i, 0, 0)),
                  pl.BlockSpec((1, 1, d, f), wmap),
                  pl.BlockSpec((1, 1, d, f), wmap),
                  pl.BlockSpec((1, 1, f, d), wmap)],
        out_specs=[pl.BlockSpec((1, rc, d), lambda i: (i, 0, 0)),
                   pl.BlockSpec((1, rs, d), lambda i: (i, 0, 0))],
        out_shape=[jax.ShapeDtypeStruct((e, rc, d), BF16),
                   jax.ShapeDtypeStruct((e, rs, d), BF16)],
        compiler_params=_cparams("arbitrary"),
        name="expert_ffn",
    )(xc, gc, xs, gs, w_gate, w_up, w_down)


def _scatter_kernel(y_ref, slot_ref, x1_ref, mod_ref, fg_ref, o_ref, *, cap, final):
    tq = x1_ref.shape[1]
    e = N_EXP
    slot = slot_ref[0]
    if e * cap <= 1024:
        lanes = lax.broadcasted_iota(I32, (tq, e * cap), 1)
        hit = None
        for ee in range(e):
            gs = jnp.where(slot[:, ee:ee + 1] >= 0, slot[:, ee:ee + 1] + ee * cap, -1)
            m = lanes == gs
            hit = m if hit is None else jnp.logical_or(hit, m)
        pt = jnp.where(hit, 1.0, 0.0).astype(BF16)
        acc = _dot(pt, y_ref[:, 0].reshape(e * cap, -1))
    else:
        lanes = lax.broadcasted_iota(I32, (tq, cap), 1)
        acc = None
        for ee in range(e):
            pt = jnp.where(lanes == slot[:, ee:ee + 1], 1.0, 0.0).astype(BF16)
            part = _dot(pt, y_ref[ee, 0])
            acc = part if acc is None else acc + part
    x2 = x1_ref[0] + mod_ref[0, 5:6, :] * acc
    if final:
        x2 = x2 * lax.rsqrt(jnp.mean(x2 * x2, -1, keepdims=True) + EPS) * fg_ref[...]
    o_ref[0] = x2


def _scatter(y, slot_col, x1, mod, final_g, cap, final):
    b, t, d = x1.shape
    e = N_EXP
    tq = min(t, 512)
    bc = mod.shape[0]
    mod_map = (lambda i, j: (i, 0, 0)) if bc > 1 else (lambda i, j: (0, 0, 0))
    return pl.pallas_call(
        functools.partial(_scatter_kernel, cap=cap, final=final),
        grid=(b, t // tq),
        in_specs=[pl.BlockSpec((e, 1, cap, d), lambda i, j: (0, i, 0, 0)),
                  pl.BlockSpec((1, tq, e), lambda i, j: (i, j, 0)),
                  pl.BlockSpec((1, tq, d), lambda i, j: (i, j, 0)),
                  pl.BlockSpec((1, 8, d), mod_map),
                  pl.BlockSpec((1, d), lambda i, j: (0, 0))],
        out_specs=pl.BlockSpec((1, tq, d), lambda i, j: (i, j, 0)),
        out_shape=jax.ShapeDtypeStruct((b, t, d), F32),
        compiler_params=_cparams("arbitrary", "arbitrary"),
        name="scatter",
    )(y, slot_col, x1, mod, final_g.reshape(1, d))


def kernel(x_prompt, x_sample, state_mlstm_C, state_mlstm_n, state_mlstm_m, state_delta_S, c, c_ctx, norm1_g, norm2_g, w_ada, b_ada, w_in, w_out, w_router, w_gate, w_up, w_down, final_g, mlstm_ig_b, mlstm_fg_b, mlstm_norm_g, delta_conv_w, delta_A_log, delta_dt_bias, delta_norm_g):
    depth = w_in.shape[0]
    d = x_prompt.shape[-1]
    nb_s = x_sample.shape[0]
    cond8 = jnp.zeros((8, d), F32).at[0].set(c_ctx).at[1:1 + nb_s].set(c)
    mods = _modulation(cond8, w_ada, b_ada).reshape(depth, 8, 6, d)
    mods = jnp.pad(mods, ((0, 0), (0, 0), (0, 2), (0, 0)))
    xs = [x_prompt, x_sample]
    conv_rows = [x_prompt.shape[1], GRID_W]
    new_states = {}
    for i in range(depth):
        mod = [mods[i, 0:1], mods[i, 1:1 + nb_s]]
        w_qkvo = w_in[i, :, :4 * d].astype(BF16)
        w_g = w_in[i, :, 4 * d:]
        w_o = w_out[i].astype(BF16)
        wr_t = w_router[i].T
        j = i // 2
        routed = []
        for p in range(2):
            x = xs[p]
            b, t, _ = x.shape
            cap = CAP_FACTOR * t // N_EXP
            if i % 2 == 0:
                q, k, v, og, gates = _in_proj(x, mod[p], norm1_g[i], w_qkvo, w_g, None, "mlstm", None)
                state = None if p == 0 else (state_mlstm_C[:, j], state_mlstm_n[:, j], state_mlstm_m[:, j])
                a, st = _mlstm(q, k, v, og, gates, mlstm_ig_b[j], mlstm_fg_b[j], mlstm_norm_g[j], state, p == 0)
                if p == 0:
                    new_states.setdefault("m", []).append(st)
            else:
                q, k, v, og, gates = _in_proj(x, mod[p], norm1_g[i], w_qkvo, w_g, delta_conv_w[j], "delta", conv_rows[p])
                state = None if p == 0 else state_delta_S[:, j]
                a, st = _delta(q, k, v, og, gates, delta_A_log[j], delta_dt_bias[j], delta_norm_g[j], state, p == 0)
                if p == 0:
                    new_states.setdefault("d", []).append(st)
            x1, h2, logits_t = _out_proj(a, w_o, x, mod[p], norm2_g[i], wr_t)
            slotm, aff_t = _route(logits_t, cap)
            xg, gsel = _gather(h2, slotm, aff_t, cap)
            routed.append((x1, slotm, xg, gsel, cap, b))
        (x1c, slc, xgc, gsc, capc, bc_), (x1s, sls, xgs, gss, caps, bs_) = routed
        yc, ys = _ffn(xgc.reshape(N_EXP, bc_ * capc, d), gsc.reshape(N_EXP, bc_ * capc, 1),
                      xgs.reshape(N_EXP, bs_ * caps, d), gss.reshape(N_EXP, bs_ * caps, 1),
                      w_gate, w_up, w_down, i)
        final = i == depth - 1
        xs = [_scatter(yc.reshape(N_EXP, bc_, capc, d), slc.transpose(0, 2, 1), x1c, mod[0], final_g, capc, final),
              _scatter(ys.reshape(N_EXP, bs_, caps, d), sls.transpose(0, 2, 1), x1s, mod[1], final_g, caps, final)]
    ms = new_states["m"]
    new_c = jnp.stack([s[0] for s in ms], 1)
    new_n = jnp.stack([s[1] for s in ms], 1)
    new_m = jnp.stack([s[2] for s in ms], 1)
    new_s = jnp.stack(new_states["d"], 1)
    return (xs[0], xs[1], new_c, new_n, new_m, new_s)
```
